```python
import math
import jax, jax.numpy as jnp
from jax import lax
import numpy as np

D_MODEL = 2048
BATCH = 2
SEQ = 8192
DEPTH = 4

GRID_W = 64
CTX_LEN = 256
NORM_EPS = 1e-6
N_MOD = 6

RW_HEADS = 8
RW_HEAD_DIM = 64
RW_WIDTH = RW_HEADS * RW_HEAD_DIM
RW_DECAY_RANK = 64
RW_AAA_RANK = 64
RW_GATE_RANK = 128
RW_LN_EPS = 64e-5

S5_WIDTH = 512
S5_GROUP = 16
S5_GROUPS = S5_WIDTH // S5_GROUP
S5_STATE = 64
S5_DT_MIN = 0.001
S5_DT_MAX = 0.1

AT_HEADS = 8
AT_KV_HEADS = 2
AT_GROUP = AT_HEADS // AT_KV_HEADS
AT_HEAD_DIM = 128
AT_WIDTH = AT_HEADS * AT_HEAD_DIM
AT_KV_WIDTH = AT_KV_HEADS * AT_HEAD_DIM
Q_BLOCK = 128
ROPE_THETA = 10000.0
ROPE_AXIS_DIM = AT_HEAD_DIM // 2

N_BRANCH = 3
MIX_WIDTH = RW_WIDTH + S5_WIDTH + AT_WIDTH

OFF_WD = 3 * RW_WIDTH
OFF_AD = OFF_WD + 2 * RW_DECAY_RANK
OFF_GD = OFF_AD + 2 * RW_AAA_RANK
OFF_S5 = OFF_GD + RW_GATE_RANK
OFF_Q = OFF_S5 + S5_WIDTH
OFF_K = OFF_Q + AT_WIDTH
OFF_V = OFF_K + AT_KV_WIDTH
OFF_GATE = OFF_V + AT_KV_WIDTH
N_IN = OFF_GATE + N_BRANCH * D_MODEL

N_EXPERTS = 16
EXPERT_FF = 1024
CAPACITY_FACTOR = 2

kernel_name = "hybrid_rwkv7_s5_gqa_ecmoe_diffusion_trunk"

F32 = jnp.float32


def rms_norm(x, g):
    x32 = x.astype(F32)
    y = x32 * lax.rsqrt(jnp.mean(x32 * x32, axis=-1, keepdims=True) + NORM_EPS)
    return (y * g.astype(F32)).astype(x.dtype)


def modulate(h, shift, scale):
    return h * (1 + scale) + shift


def short_conv(x, w):
    xp = jnp.pad(x, ((0, 0), (1, 1), (0, 0)))
    return xp[:, :-2] * w[0] + xp[:, 1:-1] * w[1] + xp[:, 2:] * w[2]


def rwkv_scan(state0, decay, key, value, kk, a, r=None, reverse=False):
    emit = r is not None
    seqs = (decay, key, value, kk, a) + ((r,) if emit else ())
    xs = tuple(jnp.moveaxis(t, 1, 0) for t in seqs)

    def step(S, inp):
        w_t, k_t, v_t, kk_t, a_t = inp[:5]
        S = (S * w_t[:, :, None, :]
             - jnp.einsum('bhij,bhj->bhi', S, kk_t)[..., None] * (kk_t * a_t)[:, :, None, :]
             + v_t[..., None] * k_t[:, :, None, :])
        out = jnp.einsum('bhij,bhj->bhi', S, inp[5]) if emit else None
        return S, out

    S, ys = lax.scan(step, state0, xs, reverse=reverse)
    return S, (jnp.moveaxis(ys, 0, 1) if emit else None)


def rwkv_prep(p, conv_w, w0, w2, a0, a2, k_k, k_a):
    B, L, _ = p.shape

    def heads(t):
        return t.reshape(B, L, RW_HEADS, RW_HEAD_DIM).astype(F32)

    rkv = short_conv(p[..., :OFF_WD], conv_w)
    r, k, v = jnp.split(rkv, 3, axis=-1)
    wd = jnp.tanh(p[..., OFF_WD:OFF_AD]).reshape(B, L, 2, RW_DECAY_RANK)
    ad = p[..., OFF_AD:OFF_GD].reshape(B, L, 2, RW_AAA_RANK)
    w_log = -jax.nn.softplus(-(w0 + jnp.einsum('blzr,zrc->blzc', wd, w2)).astype(F32)) - 0.5
    decay = jnp.exp(-jnp.exp(w_log))
    a = jax.nn.sigmoid((a0 + jnp.einsum('blzr,zrc->blzc', ad, a2)).astype(F32))
    kk = heads(k * k_k)
    kk = kk * lax.rsqrt(jnp.sum(kk * kk, axis=-1, keepdims=True) + 1e-12)
    kd = k.astype(F32)[:, :, None] * (1 + (a - 1) * k_a.astype(F32))
    dirs = [(heads(decay[:, :, z]), heads(kd[:, :, z]), heads(a[:, :, z])) for z in range(2)]
    return heads(r), heads(v), kk, dirs


def rwkv_mix(pc, pl, conv_w, w0, w2, a0, a2, g2, k_k, k_a, r_k, ln_g, ln_b, need_ctx):
    r_c, v_c, kk_c, dirs_c = rwkv_prep(pc, conv_w, w0, w2, a0, a2, k_k, k_a)
    r_l, v_l, kk_l, dirs_l = rwkv_prep(pl, conv_w, w0, w2, a0, a2, k_k, k_a)
    B = pl.shape[0]
    o_c, o_l = 0.0, 0.0
    for z in range(2):
        rev = z == 1
        S0 = jnp.zeros((B, RW_HEADS, RW_HEAD_DIM, RW_HEAD_DIM), F32)
        w_cz, k_cz, a_cz = dirs_c[z]
        S_c, y_c = rwkv_scan(S0, w_cz, k_cz, v_c, kk_c, a_cz, r_c if need_ctx else None, rev)
        w_lz, k_lz, a_lz = dirs_l[z]
        _, y_l = rwkv_scan(S_c, w_lz, k_lz, v_l, kk_l, a_lz, r_l, rev)
        o_l = o_l + y_l
        if need_ctx:
            o_c = o_c + y_c

    def finish(o, r, v, dirs, p):
        B_, L_ = o.shape[:2]
        mu = jnp.mean(o, axis=-1, keepdims=True)
        var = jnp.mean(jnp.square(o - mu), axis=-1, keepdims=True)
        o_n = ((o - mu) * lax.rsqrt(var + RW_LN_EPS)).reshape(B_, L_, RW_WIDTH)
        bonus = jnp.sum(r * (dirs[0][1] + dirs[1][1]) * r_k.astype(F32), axis=-1, keepdims=True) * v
        y = o_n * ln_g + ln_b + bonus.reshape(B_, L_, RW_WIDTH)
        g = jax.nn.sigmoid(p[..., OFF_GD:OFF_S5]) @ g2
        return (y * g).astype(p.dtype)

    y_l = finish(o_l, r_l, v_l, dirs_l, pl)
    y_c = finish(o_c, r_c, v_c, dirs_c, pc) if need_ctx else None
    return y_c, y_l


def s5_discretize(lam_re, lam_im, log_dt, b_re, b_im):
    lam_re, lam_im = lam_re.astype(F32), lam_im.astype(F32)
    dt = jnp.exp(log_dt.astype(F32))[:, None]
    mag = jnp.exp(lam_re * dt)
    ab_re, ab_im = mag * jnp.cos(lam_im * dt), mag * jnp.sin(lam_im * dt)
    den = lam_re * lam_re + lam_im * lam_im
    num_re = ab_re - 1.0
    co_re = (num_re * lam_re + ab_im * lam_im) / den
    co_im = (ab_im * lam_re - num_re * lam_im) / den
    b_re, b_im = b_re.astype(F32), b_im.astype(F32)
    bb_re = co_re[..., None] * b_re - co_im[..., None] * b_im
    bb_im = co_re[..., None] * b_im + co_im[..., None] * b_re
    return ab_re, ab_im, bb_re, bb_im


def complex_scan(ab_re, ab_im, bu_re, bu_im, reverse):
    L = bu_re.shape[0]
    a_re = jnp.broadcast_to(ab_re, (L, 1) + ab_re.shape)
    a_im = jnp.broadcast_to(ab_im, (L, 1) + ab_im.shape)

    def op(e1, e2):
        a1r, a1i, b1r, b1i = e1
        a2r, a2i, b2r, b2i = e2
        return (a2r * a1r - a2i * a1i, a2r * a1i + a2i * a1r,
                a2r * b1r - a2i * b1i + b2r, a2r * b1i + a2i * b1r + b2i)

    _, _, xr, xi = lax.associative_scan(op, (a_re, a_im, bu_re, bu_im), reverse=reverse)
    return xr, xi


def s5_readout(xr, xi, c_re, c_im):
    return jnp.einsum('lbgn,gcn->lbgc', xr, c_re) - jnp.einsum('lbgn,gcn->lbgc', xi, c_im)


def s5_mix(pc, pl, lam_re, lam_im, log_dt, b_re, b_im, c_re, c_im, d_skip, w_glu, need_ctx):
    def time_major(p):
        B, L, _ = p.shape
        u = p[..., OFF_S5:OFF_Q].astype(F32).reshape(B, L, S5_GROUPS, S5_GROUP)
        return jnp.moveaxis(u, 1, 0)

    uc, ul = time_major(pc), time_major(pl)
    y_c, y_l = 0.0, 0.0
    for z in range(2):
        rev = z == 1
        ab_re, ab_im, bb_re, bb_im = s5_discretize(lam_re[z], lam_im[z], log_dt[z], b_re[z], b_im[z])
        cr, ci = c_re[z].astype(F32), c_im[z].astype(F32)
        xc_re, xc_im = complex_scan(ab_re, ab_im,
                                    jnp.einsum('lbgc,gnc->lbgn', uc, bb_re),
                                    jnp.einsum('lbgc,gnc->lbgn', uc, bb_im), rev)
        end = 0 if rev else -1
        h_re, h_im = xc_re[end], xc_im[end]
        start = -1 if rev else 0
        lr = jnp.einsum('lbgc,gnc->lbgn', ul, bb_re).at[start].add(ab_re * h_re - ab_im * h_im)
        li = jnp.einsum('lbgc,gnc->lbgn', ul, bb_im).at[start].add(ab_re * h_im + ab_im * h_re)
        xl_re, xl_im = complex_scan(ab_re, ab_im, lr, li, rev)
        y_l = y_l + s5_readout(xl_re, xl_im, cr, ci)
        if need_ctx:
            y_c = y_c + s5_readout(xc_re, xc_im, cr, ci)

    def finish(y, u, dtype):
        L, B = y.shape[:2]
        y = y + d_skip.astype(F32).reshape(S5_GROUPS, S5_GROUP) * u
        y = jax.nn.gelu(jnp.moveaxis(y, 0, 1).reshape(B, L, S5_WIDTH))
        return (y * jax.nn.sigmoid(y @ w_glu.astype(F32))).astype(dtype)

    out_l = finish(y_l, ul, pl.dtype)
    out_c = finish(y_c, uc, pc.dtype) if need_ctx else None
    return out_c, out_l


def rope_axis(x, ang):
    half = x.shape[-1] // 2
    shape = (1, ang.shape[0]) + (1,) * (x.ndim - 3) + (ang.shape[1],)
    cos = jnp.cos(ang).reshape(shape).astype(x.dtype)
    sin = jnp.sin(ang).reshape(shape).astype(x.dtype)
    x1, x2 = x[..., :half], x[..., half:]
    return jnp.concatenate([x1 * cos - x2 * sin, x2 * cos + x1 * sin], axis=-1)


def rope_2d(x, ang_row, ang_col):
    h = x.shape[-1] // 2
    return jnp.concatenate([rope_axis(x[..., :h], ang_row), rope_axis(x[..., h:], ang_col)], axis=-1)


def attend(q, k, v):
    s = jnp.einsum('bqkgd,bskd->bkgqs', q, k).astype(F32) * (AT_HEAD_DIM ** -0.5)
    p = jax.nn.softmax(s, axis=-1).astype(v.dtype)
    return jnp.einsum('bkgqs,bskd->bqkgd', p, v)


def attn_mix(pc, pl, qn, kn, ang_row, ang_col, need_ctx):
    def kv(p):
        B, L, _ = p.shape
        k = rms_norm(p[..., OFF_K:OFF_V].reshape(B, L, AT_KV_HEADS, AT_HEAD_DIM), kn)
        v = p[..., OFF_V:OFF_GATE].reshape(B, L, AT_KV_HEADS, AT_HEAD_DIM)
        return k, v

    def queries(p):
        B, L, _ = p.shape
        return rms_norm(p[..., OFF_Q:OFF_K].reshape(B, L, AT_KV_HEADS, AT_GROUP, AT_HEAD_DIM), qn)

    B, L, _ = pl.shape
    k_c, v_c = kv(pc)
    k_l, v_l = kv(pl)
    k_l = rope_2d(k_l, ang_row, ang_col)
    q_l = rope_2d(queries(pl), ang_row, ang_col)
    k_all = jnp.concatenate([k_c, k_l], axis=1)
    v_all = jnp.concatenate([v_c, v_l], axis=1)
    nb = L // Q_BLOCK
    qb = jnp.moveaxis(q_l.reshape(B, nb, Q_BLOCK, AT_KV_HEADS, AT_GROUP, AT_HEAD_DIM), 1, 0)
    o = lax.map(lambda qq: attend(qq, k_all, v_all), qb)
    o_l = jnp.moveaxis(o, 0, 1).reshape(B, L, AT_WIDTH)
    o_c = attend(queries(pc), k_c, v_c).reshape(pc.shape[0], pc.shape[1], AT_WIDTH) if need_ctx else None
    return o_c, o_l


def merge(p, y_rw, y_s5, y_at, w_branch, w_out):
    B, L, _ = p.shape
    g = jax.nn.sigmoid(p[..., OFF_GATE:]).reshape(B, L, N_BRANCH, D_MODEL)
    m = (g[:, :, 0] * (y_rw @ w_branch[:RW_WIDTH])
         + g[:, :, 1] * (y_s5 @ w_branch[RW_WIDTH:RW_WIDTH + S5_WIDTH])
         + g[:, :, 2] * (y_at @ w_branch[RW_WIDTH + S5_WIDTH:]))
    return m @ w_out


def ec_moe(h, router, wg, wu, wd):
    B, n, D = h.shape
    cap = CAPACITY_FACTOR * n // N_EXPERTS
    aff = jax.nn.softmax((h @ router).astype(F32), axis=-1)
    gate, idx = lax.top_k(jnp.swapaxes(aff, 1, 2), cap)
    xs = jax.vmap(lambda hb, ib: hb[ib])(h, idx)
    hid = jax.nn.silu(jnp.einsum('becd,edf->becf', xs, wg)) * jnp.einsum('becd,edf->becf', xs, wu)
    y = jnp.einsum('becf,efd->becd', hid, wd) * gate[..., None].astype(h.dtype)
    return jax.vmap(lambda yb, ib: jnp.zeros((n, D), h.dtype).at[ib.reshape(-1)].add(yb.reshape(-1, D)))(y, idx)


def setup_inputs(seed: int = 0) -> dict:
    key = jax.random.key(seed)
    ks = iter(jax.random.split(key, 48))

    def nrm(shape, scale):
        return jax.random.normal(next(ks), shape, F32) * scale

    D, C, H, N = D_MODEL, RW_WIDTH, RW_HEADS, RW_HEAD_DIM
    G, S, E, Fe = S5_GROUPS, S5_STATE, N_EXPERTS, EXPERT_FF
    x = nrm((BATCH, SEQ, D), 1.0)
    c = nrm((BATCH, D), 1.0)
    ctx = nrm((BATCH, CTX_LEN, D), 1.0)
    c_ctx = nrm((D,), 1.0)
    mod_w = nrm((DEPTH, D, N_MOD * D), 0.5 * D ** -0.5)
    mod_b = nrm((DEPTH, N_MOD * D), 0.02)
    norm1_g = 1.0 + nrm((DEPTH, D), 0.02)
    norm2_g = 1.0 + nrm((DEPTH, D), 0.02)
    w_in = nrm((DEPTH, D, N_IN), D ** -0.5)
    rwkv_conv = jnp.array([0.25, 0.5, 0.25], F32)[None, :, None] + nrm((DEPTH, 3, 3 * C), 0.05)
    ch = (jnp.arange(C) % N).astype(F32) / (N - 1)
    rwkv_w0 = (-6.0 + 5.0 * ch) + nrm((DEPTH, 2, C), 0.1)
    rwkv_w2 = nrm((DEPTH, 2, RW_DECAY_RANK, C), 0.1 * RW_DECAY_RANK ** -0.5)
    rwkv_a0 = nrm((DEPTH, 2, C), 0.1)
    rwkv_a2 = nrm((DEPTH, 2, RW_AAA_RANK, C), 0.1 * RW_AAA_RANK ** -0.5)
    rwkv_g2 = nrm((DEPTH, RW_GATE_RANK, C), RW_GATE_RANK ** -0.5)
    rwkv_kk = 0.85 + nrm((DEPTH, C), 0.02)
    rwkv_ka = 1.0 + nrm((DEPTH, C), 0.02)
    rwkv_rk = nrm((DEPTH, H, N), 0.1)
    rwkv_ln_g = 1.0 + nrm((DEPTH, C), 0.02)
    rwkv_ln_b = nrm((DEPTH, C), 0.02)
    s5_lam_re = -0.5 + nrm((DEPTH, 2, G, S), 0.01)
    s5_lam_im = math.pi * jnp.arange(S, dtype=F32) + nrm((DEPTH, 2, G, S), 0.01)
    s5_log_dt = math.log(S5_DT_MIN) + (math.log(S5_DT_MAX) - math.log(S5_DT_MIN)) * jax.random.uniform(next(ks), (DEPTH, 2, G), F32)
    s5_b_re = nrm((DEPTH, 2, G, S, S5_GROUP), (2 * S5_GROUP) ** -0.5)
    s5_b_im = nrm((DEPTH, 2, G, S, S5_GROUP), (2 * S5_GROUP) ** -0.5)
    s5_c_re = nrm((DEPTH, 2, G, S5_GROUP, S), (2 * S) ** -0.5)
    s5_c_im = nrm((DEPTH, 2, G, S5_GROUP, S), (2 * S) ** -0.5)
    s5_d = nrm((DEPTH, S5_WIDTH), 1.0)
    s5_glu = nrm((DEPTH, S5_WIDTH, S5_WIDTH), S5_WIDTH ** -0.5)
    attn_qn = 1.0 + nrm((DEPTH, AT_HEAD_DIM), 0.02)
    attn_kn = 1.0 + nrm((DEPTH, AT_HEAD_DIM), 0.02)
    w_branch = jnp.concatenate([nrm((DEPTH, RW_WIDTH, D), RW_WIDTH ** -0.5),
                                nrm((DEPTH, S5_WIDTH, D), S5_WIDTH ** -0.5),
                                nrm((DEPTH, AT_WIDTH, D), AT_WIDTH ** -0.5)], axis=1)
    w_out = nrm((DEPTH, D, D), D ** -0.5)
    router = nrm((DEPTH, D, E), D ** -0.5)
    exp_gate = nrm((DEPTH, E, D, Fe), D ** -0.5)
    exp_up = nrm((DEPTH, E, D, Fe), D ** -0.5)
    exp_down = nrm((DEPTH, E, Fe, D), Fe ** -0.5)
    final_g = 1.0 + nrm((D,), 0.02)
    return {"x": x, "c": c, "ctx": ctx, "c_ctx": c_ctx, "mod_w": mod_w, "mod_b": mod_b,
            "norm1_g": norm1_g, "norm2_g": norm2_g, "w_in": w_in, "rwkv_conv": rwkv_conv,
            "rwkv_w0": rwkv_w0, "rwkv_w2": rwkv_w2, "rwkv_a0": rwkv_a0, "rwkv_a2": rwkv_a2,
            "rwkv_g2": rwkv_g2, "rwkv_kk": rwkv_kk, "rwkv_ka": rwkv_ka, "rwkv_rk": rwkv_rk,
            "rwkv_ln_g": rwkv_ln_g, "rwkv_ln_b": rwkv_ln_b, "s5_lam_re": s5_lam_re,
            "s5_lam_im": s5_lam_im, "s5_log_dt": s5_log_dt, "s5_b_re": s5_b_re, "s5_b_im": s5_b_im,
            "s5_c_re": s5_c_re, "s5_c_im": s5_c_im, "s5_d": s5_d, "s5_glu": s5_glu,
            "attn_qn": attn_qn, "attn_kn": attn_kn, "w_branch": w_branch, "w_out": w_out,
            "router": router, "exp_gate": exp_gate, "exp_up": exp_up, "exp_down": exp_down,
            "final_g": final_g}


def reference(x, c, ctx, c_ctx, mod_w, mod_b, norm1_g, norm2_g, w_in, rwkv_conv,
              rwkv_w0, rwkv_w2, rwkv_a0, rwkv_a2, rwkv_g2, rwkv_kk, rwkv_ka, rwkv_rk,
              rwkv_ln_g, rwkv_ln_b, s5_lam_re, s5_lam_im, s5_log_dt, s5_b_re, s5_b_im,
              s5_c_re, s5_c_im, s5_d, s5_glu, attn_qn, attn_kn, w_branch, w_out,
              router, exp_gate, exp_up, exp_down, final_g):
    B, L, D = x.shape
    rows = L // GRID_W
    row = jnp.repeat(jnp.arange(rows, dtype=F32), GRID_W)
    col = jnp.tile(jnp.arange(GRID_W, dtype=F32), rows)
    freqs = ROPE_THETA ** (-jnp.arange(ROPE_AXIS_DIM // 2, dtype=F32) / (ROPE_AXIS_DIM // 2))
    ang_row = row[:, None] * freqs
    ang_col = col[:, None] * freqs

    sc = jax.nn.silu(c)
    scc = jax.nn.silu(c_ctx)
    xl, xc = x, ctx
    for l in range(DEPTH):
        need_ctx = l < DEPTH - 1
        mod_l = (sc @ mod_w[l] + mod_b[l]).reshape(B, 1, N_MOD, D)
        mod_c = (scc @ mod_w[l] + mod_b[l]).reshape(1, 1, N_MOD, D)
        hl = modulate(rms_norm(xl, norm1_g[l]), mod_l[:, :, 0], mod_l[:, :, 1])
        hc = modulate(rms_norm(xc, norm1_g[l]), mod_c[:, :, 0], mod_c[:, :, 1])
        pl = hl @ w_in[l]
        pc = hc @ (w_in[l] if need_ctx else w_in[l][:, :OFF_GATE])
        rw_c, rw_l = rwkv_mix(pc, pl, rwkv_conv[l], rwkv_w0[l], rwkv_w2[l], rwkv_a0[l], rwkv_a2[l],
                              rwkv_g2[l], rwkv_kk[l], rwkv_ka[l], rwkv_rk[l], rwkv_ln_g[l],
                              rwkv_ln_b[l], need_ctx)
        s5_c, s5_l = s5_mix(pc, pl, s5_lam_re[l], s5_lam_im[l], s5_log_dt[l], s5_b_re[l], s5_b_im[l],
                            s5_c_re[l], s5_c_im[l], s5_d[l], s5_glu[l], need_ctx)
        at_c, at_l = attn_mix(pc, pl, attn_qn[l], attn_kn[l], ang_row, ang_col, need_ctx)
        xl = xl + mod_l[:, :, 2] * merge(pl, rw_l, s5_l, at_l, w_branch[l], w_out[l])
        hl2 = modulate(rms_norm(xl, norm2_g[l]), mod_l[:, :, 3], mod_l[:, :, 4])
        xl = xl + mod_l[:, :, 5] * ec_moe(hl2, router[l], exp_gate[l], exp_up[l], exp_down[l])
        if need_ctx:
            xc = xc + mod_c[:, :, 2] * merge(pc, rw_c, s5_c, at_c, w_branch[l], w_out[l])
            hc2 = modulate(rms_norm(xc, norm2_g[l]), mod_c[:, :, 3], mod_c[:, :, 4])
            xc = xc + mod_c[:, :, 5] * ec_moe(hc2, router[l], exp_gate[l], exp_up[l], exp_down[l])
    return rms_norm(xl, final_g)
```

```python
import functools
import math

import jax
import jax.numpy as jnp
from jax import lax
from jax.experimental import pallas as pl
from jax.experimental.pallas import tpu as pltpu

F32 = jnp.float32
BF16 = jnp.bfloat16

NORM_EPS = 1e-6
N_MOD = 6
GRID_W = 64

RW_HEADS = 8
RW_HEAD_DIM = 64
RW_WIDTH = 512
RW_RANK = 64
RW_GATE_RANK = 128
RW_LN_EPS = 64e-5

S5_WIDTH = 512
S5_GROUP = 16
S5_GROUPS = 32
S5_STATE = 64
S5_NX = S5_GROUPS * S5_STATE
S5_PACKS = 4

AT_HEADS = 8
AT_KV_HEADS = 2
AT_GROUP = AT_HEADS // AT_KV_HEADS
AT_HEAD_DIM = 128
AT_WIDTH = 1024
AT_KV_WIDTH = 256
ROPE_THETA = 10000.0

N_EXPERTS = 16
CAPACITY_FACTOR = 2

OFF_WD = 3 * RW_WIDTH
OFF_S5 = OFF_WD + 4 * RW_RANK + RW_GATE_RANK
OFF_Q = OFF_S5 + S5_WIDTH
OFF_K = OFF_Q + AT_WIDTH
OFF_GATE = OFF_K + 2 * AT_KV_WIDTH

P_RW = 0
P_S5 = 2048
P_KV = 2560
P_Q = 3072
P_GATE = 4096

VMEM_LIMIT = 56 * 1024 * 1024


def _params(n_axes, vmem=VMEM_LIMIT):
    return pltpu.CompilerParams(dimension_semantics=("arbitrary",) * n_axes, vmem_limit_bytes=vmem)


def _pick(n, cands):
    for t in cands:
        if n % t == 0:
            return t
    raise ValueError(f"no tile for {n}")


def _split3(x):
    hi = x.astype(BF16)
    r1 = x - hi.astype(F32)
    mid = r1.astype(BF16)
    lo = (r1 - mid.astype(F32)).astype(BF16)
    return hi, mid, lo


def _dot_exact_rhs(x, w_bf16, pieces=3):
    parts = _split3(x)[:pieces]
    acc = jnp.dot(parts[0], w_bf16, preferred_element_type=F32)
    for p in parts[1:]:
        acc = acc + jnp.dot(p, w_bf16, preferred_element_type=F32)
    return acc


def _dot_f32(a, b, dims=(((1,), (0,)), ((), ()))):
    a1, a2, a3 = _split3(a)
    b1, b2, b3 = _split3(b)
    d = functools.partial(lax.dot_general, dimension_numbers=dims, preferred_element_type=F32)
    return (d(a1, b1) + (d(a1, b2) + d(a2, b1))) + ((d(a1, b3) + d(a3, b1)) + d(a2, b2))


def _sigmoid(x):
    return 1.0 / (1.0 + jnp.exp(-x))


def _silu(x):
    return x * _sigmoid(x)


def _row_is_ctx(i, tiles_per_batch, tm, c_len):
    pos = (i % tiles_per_batch) * tm + lax.broadcasted_iota(jnp.int32, (tm, 1), 0)
    return pos < c_len


def _mod_rows(mod_ref, b, n_batch, which, d, is_ctx):
    lat = mod_ref[pl.ds(b, 1), which * d:(which + 1) * d]
    ctx = mod_ref[n_batch:n_batch + 1, which * d:(which + 1) * d]
    return jnp.where(is_ctx, ctx, lat)


def _rms(x):
    return x * lax.rsqrt(jnp.mean(x * x, axis=-1, keepdims=True) + NORM_EPS)


def _mod_kernel(ct_ref, w_ref, b_ref, o_ref, *, n_rows):
    w = w_ref[0]
    for m in range(n_rows):
        col = ct_ref[:, m:m + 1]
        o_ref[0, m:m + 1, :] = jnp.sum(w * _silu(col), axis=0, keepdims=True) + b_ref[0]


def _modulation(c_all_t, mod_w, mod_b):
    depth, d, n6 = mod_w.shape
    n_rows = c_all_t.shape[1]
    tn = 512
    return pl.pallas_call(
        functools.partial(_mod_kernel, n_rows=n_rows),
        grid=(depth, n6 // tn),
        in_specs=[pl.BlockSpec((d, n_rows), lambda l, n: (0, 0)),
                  pl.BlockSpec((1, d, tn), lambda l, n: (l, 0, n)),
                  pl.BlockSpec((1, 1, tn), lambda l, n: (l, 0, n))],
        out_specs=pl.BlockSpec((1, n_rows, tn), lambda l, n: (l, 0, n)),
        out_shape=jax.ShapeDtypeStruct((depth, n_rows, n6), F32),
        compiler_params=_params(2), name="modulation",
    )(c_all_t, mod_w, mod_b.reshape(depth, 1, n6))


def _inproj_kernel(x_ref, mod_ref, g_ref, w_ref, o_ref, h_ref, *, tpb, tm, c_len, n_batch, d):
    i = pl.program_id(0)

    @pl.when(pl.program_id(1) == 0)
    def _():
        b = i // tpb
        n_chunks = next(n for n in (4, 3, 2, 1) if tm % (16 * n) == 0)
        rc = tm // n_chunks
        for j in range(n_chunks):
            rows = slice(j * rc, (j + 1) * rc)
            is_ctx = _row_is_ctx(i, tpb, tm, c_len)[rows]
            shift = _mod_rows(mod_ref, b, n_batch, 0, d, is_ctx)
            scale = _mod_rows(mod_ref, b, n_batch, 1, d, is_ctx)
            h = _rms(x_ref[rows, :]) * g_ref[...]
            h_ref[rows, :] = (h * (1 + scale) + shift).astype(BF16)

    o_ref[...] = jnp.dot(h_ref[...], w_ref[...], preferred_element_type=F32)


def _inproj(x, mod, g, w_perm, *, tb, c_len, n_batch):
    m, d = x.shape
    n_out = w_perm.shape[1]
    tm = _pick(tb, (1056, 1024, 768, 640, 512, 256, 128))
    tn = 1024
    kern = functools.partial(_inproj_kernel, tpb=tb // tm, tm=tm, c_len=c_len, n_batch=n_batch, d=d)
    return pl.pallas_call(
        kern, grid=(m // tm, n_out // tn),
        in_specs=[pl.BlockSpec((tm, d), lambda i, n: (i, 0)),
                  pl.BlockSpec(mod.shape, lambda i, n: (0, 0)),
                  pl.BlockSpec((1, d), lambda i, n: (0, 0)),
                  pl.BlockSpec((d, tn), lambda i, n: (0, n))],
        out_specs=pl.BlockSpec((tm, tn), lambda i, n: (i, n)),
        out_shape=jax.ShapeDtypeStruct((m, n_out), F32),
        scratch_shapes=[pltpu.VMEM((tm, d), BF16)],
        compiler_params=_params(2), name="inproj",
    )(x, mod, g.reshape(1, d), w_perm)


def _softplus(z):
    return jnp.maximum(z, 0.0) + jnp.log(1.0 + jnp.exp(-jnp.abs(z)))


def _rwkv_prep_kernel(p_ref, prev_ref, next_ref, conv_ref, w0_ref, w2_ref, a0_ref, a2_ref, g2_ref,
                      kk_ref, ka_ref, rk_ref, ones_ref,
                      r_o, v_o, kk_o, w0_o, kd0_o, b0_o, w1_o, kd1_o, b1_o, bonus_o, g_o,
                      *, tr, tpb, c_len, tb):
    i = pl.program_id(0)
    c = RW_WIDTH
    pos = (i % tpb) * tr + lax.broadcasted_iota(jnp.int32, (tr, 1), 0)
    row = lax.broadcasted_iota(jnp.int32, (tr, 1), 0)
    first = (pos == 0) | (pos == c_len)
    last = (pos == c_len - 1) | (pos == tb - 1)

    x = p_ref[:, 0:3 * c]
    x_prev = jnp.where(row == 0, prev_ref[7:8, 0:3 * c], pltpu.roll(x, 1, 0))
    x_next = jnp.where(row == tr - 1, next_ref[0:1, 0:3 * c], pltpu.roll(x, tr - 1, 0))
    x_prev = jnp.where(first, 0.0, x_prev)
    x_next = jnp.where(last, 0.0, x_next)
    rkv = x_prev * conv_ref[0:1, :] + x * conv_ref[1:2, :] + x_next * conv_ref[2:3, :]
    r, k, v = rkv[:, 0:c], rkv[:, c:2 * c], rkv[:, 2 * c:3 * c]
    ones = ones_ref[...]

    kk = k * kk_ref[...]
    kk = kk * lax.rsqrt(_dot_exact_rhs(kk * kk, ones) + 1e-12)
    r_o[...] = r
    v_o[...] = v
    kk_o[...] = kk

    kd_sum = jnp.zeros_like(k)
    for z, (w_o, kd_o, b_o) in enumerate(((w0_o, kd0_o, b0_o), (w1_o, kd1_o, b1_o))):
        wd = jnp.tanh(p_ref[:, OFF_WD + z * RW_RANK:OFF_WD + (z + 1) * RW_RANK])
        ad = p_ref[:, OFF_WD + (2 + z) * RW_RANK:OFF_WD + (3 + z) * RW_RANK]
        w_lin = w0_ref[z:z + 1, :] + _dot_f32(wd, w2_ref[z])
        w_log = -_softplus(-w_lin) - 0.5
        w_o[...] = jnp.exp(-jnp.exp(w_log))
        a = _sigmoid(a0_ref[z:z + 1, :] + _dot_f32(ad, a2_ref[z]))
        kd = k * (1 + (a - 1) * ka_ref[...])
        kd_o[...] = kd
        b_o[...] = kk * a
        kd_sum = kd_sum + kd

    bonus_o[...] = _dot_exact_rhs(r * kd_sum * rk_ref[...], ones) * v
    gd = _sigmoid(p_ref[:, OFF_WD + 4 * RW_RANK:OFF_WD + 4 * RW_RANK + RW_GATE_RANK])
    g_o[...] = _dot_f32(gd, g2_ref[...])


def _rwkv_prep(p, conv_w, w0, w2, a0, a2, g2, k_k, k_a, r_k, ones_bd, *, tb, c_len):
    m = p.shape[0]
    c = RW_WIDTH
    tr = _pick(tb, (528, 512, 320, 256, 128))
    nblk8 = m // 8
    kern = functools.partial(_rwkv_prep_kernel, tr=tr, tpb=tb // tr, c_len=c_len, tb=tb)
    full = lambda a: pl.BlockSpec(a.shape, lambda i: (0,) * a.ndim)
    args = (conv_w, w0, w2, a0, a2, g2, k_k.reshape(1, c), k_a.reshape(1, c), r_k.reshape(1, c), ones_bd)
    out = jax.ShapeDtypeStruct((m, c), F32)
    return pl.pallas_call(
        kern, grid=(m // tr,),
        in_specs=[pl.BlockSpec((tr, 2048), lambda i: (i, 0)),
                  pl.BlockSpec((8, 2048), lambda i: (jnp.maximum(i * (tr // 8) - 1, 0), 0)),
                  pl.BlockSpec((8, 2048), lambda i: (jnp.minimum((i + 1) * (tr // 8), nblk8 - 1), 0))]
                 + [full(a) for a in args],
        out_specs=[pl.BlockSpec((tr, c), lambda i: (i, 0))] * 11,
        out_shape=[out] * 11,
        compiler_params=_params(1), name="rwkv_prep",
    )(p, p, p, *args)


def _rev_chunk(c, n_ctx_chunks, n_chunks):
    return jnp.where(c < n_ctx_chunks, n_ctx_chunks - 1 - c, n_chunks - 1 - (c - n_ctx_chunks))


def _rwkv_scan_kernel(r0, v0, k0, w0, d0, b0, r1, v1, k1, w1, d1, b1, ones_ref, o0, o1, s_ref, *, tc):
    @pl.when(pl.program_id(1) == 0)
    def _():
        s_ref[...] = jnp.zeros_like(s_ref)

    ones = ones_ref[...]
    n_hd = RW_HEAD_DIM
    diag = (lax.broadcasted_iota(jnp.int32, (n_hd, 2 * n_hd), 1) % n_hd
            == lax.broadcasted_iota(jnp.int32, (n_hd, 2 * n_hd), 0))
    dirs = ((r0, v0, k0, w0, d0, b0, o0), (r1, v1, k1, w1, d1, b1, o1))

    sub = lax.broadcasted_iota(jnp.int32, (8, 2 * n_hd), 0)
    n_groups = tc // 8

    def group(gi, carry):
        for z, (r, v, kk, w, kd, bb, o) in enumerate(dirs):
            base = pl.multiple_of((gi if z == 0 else n_groups - 1 - gi) * 8, 8)
            tiles = [ref[pl.ds(base, 8), :] for ref in (r, v, kk, w, kd, bb)]
            o_tiles = [jnp.zeros((8, 2 * n_hd), F32) for _ in range(RW_WIDTH // (2 * n_hd))]
            for jj in range(8):
                j = jj if z == 0 else 7 - jj
                for hp in range(RW_WIDTH // (2 * n_hd)):
                    sl = slice(2 * n_hd * hp, 2 * n_hd * (hp + 1))
                    r_row, v_row, a_row, w_row, k_row, b_row = (x[j:j + 1, sl] for x in tiles)
                    s = s_ref[z, hp]
                    sk = _dot_exact_rhs(s * a_row, ones)
                    vb = _dot_exact_rhs(jnp.where(diag, v_row, 0.0), ones)
                    s = s * w_row - sk * b_row + vb * k_row
                    s_ref[z, hp] = s
                    ob = _dot_exact_rhs(s * r_row, ones, pieces=2)
                    o_row = jnp.sum(jnp.where(diag, ob, 0.0), axis=0, keepdims=True)
                    o_tiles[hp] = jnp.where(sub == j, o_row, o_tiles[hp])
            for hp, o_tile in enumerate(o_tiles):
                o[pl.ds(base, 8), 2 * n_hd * hp:2 * n_hd * (hp + 1)] = o_tile
        return carry

    lax.fori_loop(0, n_groups, group, 0)


def _rwkv_scan(r, v, kk, w0, kd0, b0, w1, kd1, b1, ones_pair, *, n_batch, tb, c_len, tc):
    m, c = r.shape
    nc, ncc = tb // tc, c_len // tc
    fwd = pl.BlockSpec((tc, c), lambda b, j: (b * nc + j, 0))
    rev = pl.BlockSpec((tc, c), lambda b, j: (b * nc + _rev_chunk(j, ncc, nc), 0))
    out = jax.ShapeDtypeStruct((m, c), F32)
    return pl.pallas_call(
        functools.partial(_rwkv_scan_kernel, tc=tc), grid=(n_batch, nc),
        in_specs=[fwd] * 6 + [rev] * 6 + [pl.BlockSpec(ones_pair.shape, lambda b, j: (0, 0))],
        out_specs=[fwd, rev], out_shape=[out, out],
        scratch_shapes=[pltpu.VMEM((2, c // (2 * RW_HEAD_DIM), RW_HEAD_DIM, 2 * RW_HEAD_DIM), F32)],
        compiler_params=_params(2), name="rwkv_scan",
    )(r, v, kk, w0, kd0, b0, r, v, kk, w1, kd1, b1, ones_pair)


def _rwkv_finish_kernel(o0, o1, bonus, g, lng, lnb, ones_ref, y_ref):
    o = o0[...] + o1[...]
    ones = ones_ref[...]
    mu = _dot_exact_rhs(o, ones) * (1.0 / RW_HEAD_DIM)
    dev = o - mu
    var = _dot_exact_rhs(dev * dev, ones) * (1.0 / RW_HEAD_DIM)
    o_n = dev * lax.rsqrt(var + RW_LN_EPS)
    y_ref[...] = ((o_n * lng[...] + lnb[...] + bonus[...]) * g[...]).astype(BF16)


def _rwkv_finish(o0, o1, bonus, g, ln_g, ln_b, ones_bd, *, tb):
    m, c = o0.shape
    tr = _pick(tb, (1056, 1024, 640, 512, 256, 128))
    blk = pl.BlockSpec((tr, c), lambda i: (i, 0))
    vec = pl.BlockSpec((1, c), lambda i: (0, 0))
    return pl.pallas_call(
        _rwkv_finish_kernel, grid=(m // tr,),
        in_specs=[blk] * 4 + [vec, vec, pl.BlockSpec(ones_bd.shape, lambda i: (0, 0))],
        out_specs=blk, out_shape=jax.ShapeDtypeStruct((m, c), BF16),
        compiler_params=_params(1), name="rwkv_finish",
    )(o0, o1, bonus, g, ln_g.reshape(1, c), ln_b.reshape(1, c), ones_bd)


def _s5_discretize(lam_re, lam_im, log_dt, b_re, b_im):
    dt = jnp.exp(log_dt)[:, None]
    mag = jnp.exp(lam_re * dt)
    ab_re, ab_im = mag * jnp.cos(lam_im * dt), mag * jnp.sin(lam_im * dt)
    den = lam_re * lam_re + lam_im * lam_im
    num_re = ab_re - 1.0
    co_re = (num_re * lam_re + ab_im * lam_im) / den
    co_im = (ab_im * lam_re - num_re * lam_im) / den
    bb_re = co_re[..., None] * b_re - co_im[..., None] * b_im
    bb_im = co_re[..., None] * b_im + co_im[..., None] * b_re
    return ab_re, ab_im, bb_re, bb_im


def _s5_pack_in(bb):
    gpp = S5_GROUPS // S5_PACKS
    bb = bb.reshape(S5_PACKS, gpp, S5_STATE, S5_GROUP)
    eye = jnp.eye(gpp, dtype=bb.dtype)
    t = jnp.einsum('qgnc,gh->qgchn', bb, eye)
    return t.reshape(S5_PACKS, gpp * S5_GROUP, gpp * S5_STATE)


def _s5_pack_out(cc):
    gpp = S5_GROUPS // S5_PACKS
    cc = cc.reshape(S5_PACKS, gpp, S5_GROUP, S5_STATE)
    eye = jnp.eye(gpp, dtype=cc.dtype)
    t = jnp.einsum('qgcn,gh->qgnhc', cc, eye)
    return t.reshape(S5_PACKS, gpp * S5_STATE, gpp * S5_GROUP)


def _s5_scan_kernel(u0, u1, a_ref, bre, bim, cre, cim, y0, y1, xr_ref, xi_ref, st_ref, *, tc):
    @pl.when(pl.program_id(1) == 0)
    def _():
        st_ref[...] = jnp.zeros_like(st_ref)

    cw = S5_WIDTH // S5_PACKS
    nw = S5_NX // S5_PACKS
    for z, (u, y) in enumerate(((u0, y0), (u1, y1))):
        ub = u[...].astype(BF16)
        for q in range(S5_PACKS):
            uq = ub[:, q * cw:(q + 1) * cw]
            xr_ref[:, q * nw:(q + 1) * nw] = jnp.dot(uq, bre[z, q], preferred_element_type=F32)
            xi_ref[:, q * nw:(q + 1) * nw] = jnp.dot(uq, bim[z, q], preferred_element_type=F32)
        ar = a_ref[2 * z:2 * z + 1, :]
        ai = a_ref[2 * z + 1:2 * z + 2, :]

        def group(gi, carry, z=z, ar=ar, ai=ai):
            sr, si = carry
            base = pl.multiple_of((gi if z == 0 else tc // 8 - 1 - gi) * 8, 8)
            bur, bui = xr_ref[pl.ds(base, 8), :], xi_ref[pl.ds(base, 8), :]
            sub = lax.broadcasted_iota(jnp.int32, bur.shape, 0)
            xr_t, xi_t = bur, bui
            for jj in range(8):
                j = jj if z == 0 else 7 - jj
                sr, si = ar * sr - ai * si + bur[j:j + 1, :], ar * si + ai * sr + bui[j:j + 1, :]
                xr_t = jnp.where(sub == j, sr, xr_t)
                xi_t = jnp.where(sub == j, si, xi_t)
            xr_ref[pl.ds(base, 8), :] = xr_t
            xi_ref[pl.ds(base, 8), :] = xi_t
            return sr, si

        sr, si = lax.fori_loop(0, tc // 8, group, (st_ref[2 * z:2 * z + 1, :], st_ref[2 * z + 1:2 * z + 2, :]))
        st_ref[2 * z:2 * z + 1, :] = sr
        st_ref[2 * z + 1:2 * z + 2, :] = si
        xr = xr_ref[...].astype(BF16)
        xi = xi_ref[...].astype(BF16)
        for q in range(S5_PACKS):
            y[:, q * cw:(q + 1) * cw] = (
                jnp.dot(xr[:, q * nw:(q + 1) * nw], cre[z, q], preferred_element_type=F32)
                - jnp.dot(xi[:, q * nw:(q + 1) * nw], cim[z, q], preferred_element_type=F32))


def _s5_scan(p, a_all, bre, bim, cre, cim, *, n_batch, tb, c_len, tc):
    m = p.shape[0]
    nc, ncc = tb // tc, c_len // tc
    ucol = P_S5 // S5_WIDTH
    full = lambda a: pl.BlockSpec(a.shape, lambda b, j: (0,) * a.ndim)
    out = jax.ShapeDtypeStruct((m, S5_WIDTH), F32)
    return pl.pallas_call(
        functools.partial(_s5_scan_kernel, tc=tc), grid=(n_batch, nc),
        in_specs=[pl.BlockSpec((tc, S5_WIDTH), lambda b, j: (b * nc + j, ucol)),
                  pl.BlockSpec((tc, S5_WIDTH), lambda b, j: (b * nc + _rev_chunk(j, ncc, nc), ucol)),
                  full(a_all), full(bre), full(bim), full(cre), full(cim)],
        out_specs=[pl.BlockSpec((tc, S5_WIDTH), lambda b, j: (b * nc + j, 0)),
                   pl.BlockSpec((tc, S5_WIDTH), lambda b, j: (b * nc + _rev_chunk(j, ncc, nc), 0))],
        out_shape=[out, out],
        scratch_shapes=[pltpu.VMEM((tc, S5_NX), F32), pltpu.VMEM((tc, S5_NX), F32), pltpu.VMEM((4, S5_NX), F32)],
        compiler_params=_params(2), name="s5_scan",
    )(p, p, a_all, bre, bim, cre, cim)


def _s5_finish_kernel(y0, y1, u, d_ref, w_ref, o_ref):
    y = y0[...] + y1[...] + d_ref[...] * u[...]
    ge = 0.5 * y * (1.0 + jnp.tanh(math.sqrt(2.0 / math.pi) * (y + 0.044715 * (y * y * y))))
    gate = _sigmoid(jnp.dot(ge.astype(BF16), w_ref[...], preferred_element_type=F32))
    o_ref[...] = (ge * gate).astype(BF16)


def _s5_finish(y0, y1, p, d_skip, w_glu, *, tb):
    m, c = y0.shape
    tr = _pick(tb, (1056, 1024, 640, 512, 256, 128))
    blk = pl.BlockSpec((tr, c), lambda i: (i, 0))
    return pl.pallas_call(
        _s5_finish_kernel, grid=(m // tr,),
        in_specs=[blk, blk, pl.BlockSpec((tr, c), lambda i: (i, P_S5 // c)),
                  pl.BlockSpec((1, c), lambda i: (0, 0)), pl.BlockSpec((c, c), lambda i: (0, 0))],
        out_specs=blk, out_shape=jax.ShapeDtypeStruct((m, c), BF16),
        compiler_params=_params(1), name="s5_finish",
    )(y0, y1, p, d_skip.reshape(1, c), w_glu)


def _rope(x, cos, sin_signed):
    lane = lax.broadcasted_iota(jnp.int32, x.shape, 1)
    swapped = jnp.where(lane % 64 < 32, pltpu.roll(x, 96, 1), pltpu.roll(x, 32, 1))
    return x * cos + swapped * sin_signed


def _attn_prep_kernel(kv_ref, q_ref, cos_ref, sin_ref, qn_ref, kn_ref, qo, ko, vo):
    cos, sin = cos_ref[...], sin_ref[...]
    hd = AT_HEAD_DIM
    for h in range(AT_HEADS):
        q = _rms(q_ref[:, h * hd:(h + 1) * hd]) * qn_ref[...]
        qo[:, h * hd:(h + 1) * hd] = (_rope(q, cos, sin) * (hd ** -0.5)).astype(BF16)
    for h in range(AT_KV_HEADS):
        k = _rms(kv_ref[:, h * hd:(h + 1) * hd]) * kn_ref[...]
        ko[:, h * hd:(h + 1) * hd] = _rope(k, cos, sin).astype(BF16)
    vo[...] = kv_ref[:, AT_KV_WIDTH:2 * AT_KV_WIDTH].astype(BF16)


def _attn_prep(p, cos_t, sin_t, qn, kn, *, tb):
    m = p.shape[0]
    tr = _pick(tb, (528, 512, 320, 256, 128))
    tpb = tb // tr
    hd = AT_HEAD_DIM
    return pl.pallas_call(
        _attn_prep_kernel, grid=(m // tr,),
        in_specs=[pl.BlockSpec((tr, 2 * AT_KV_WIDTH), lambda i: (i, P_KV // (2 * AT_KV_WIDTH))),
                  pl.BlockSpec((tr, AT_WIDTH), lambda i: (i, P_Q // AT_WIDTH)),
                  pl.BlockSpec((tr, hd), lambda i: (i % tpb, 0)),
                  pl.BlockSpec((tr, hd), lambda i: (i % tpb, 0)),
                  pl.BlockSpec((1, hd), lambda i: (0, 0)), pl.BlockSpec((1, hd), lambda i: (0, 0))],
        out_specs=[pl.BlockSpec((tr, AT_WIDTH), lambda i: (i, 0)),
                   pl.BlockSpec((tr, AT_KV_WIDTH), lambda i: (i, 0)),
                   pl.BlockSpec((tr, AT_KV_WIDTH), lambda i: (i, 0))],
        out_shape=[jax.ShapeDtypeStruct((m, AT_WIDTH), BF16), jax.ShapeDtypeStruct((m, AT_KV_WIDTH), BF16),
                   jax.ShapeDtypeStruct((m, AT_KV_WIDTH), BF16)],
        compiler_params=_params(1), name="attn_prep",
    )(p, p, cos_t, sin_t, qn.reshape(1, hd), kn.reshape(1, hd))


def _flash_kernel(q_ref, k_ref, v_ref, o_ref, m_ref, l_ref, acc_ref, *, tq, tk, c_len, nk):
    qi, ki = pl.program_id(2), pl.program_id(3)

    @pl.when(ki == 0)
    def _():
        m_ref[...] = jnp.full_like(m_ref, -1e30)
        l_ref[...] = jnp.zeros_like(l_ref)
        acc_ref[...] = jnp.zeros_like(acc_ref)

    def body(masked):
        s = lax.dot_general(q_ref[...], k_ref[...], (((1,), (1,)), ((), ())), preferred_element_type=F32)
        if masked:
            row = qi * tq + lax.broadcasted_iota(jnp.int32, (tq, 1), 0)
            col = ki * tk + lax.broadcasted_iota(jnp.int32, (1, tk), 1)
            s = jnp.where((row < c_len) & (col >= c_len), -1e30, s)
        m_prev = m_ref[...]
        m_new = jnp.maximum(m_prev, jnp.max(s, axis=-1, keepdims=True))
        alpha = jnp.exp(m_prev - m_new)
        p = jnp.exp(s - m_new)
        l_ref[...] = alpha * l_ref[...] + jnp.sum(p, axis=-1, keepdims=True)
        acc_ref[...] = alpha * acc_ref[...] + jnp.dot(p.astype(BF16), v_ref[...], preferred_element_type=F32)
        m_ref[...] = m_new

    has_ctx = qi * tq < c_len
    pl.when(has_ctx)(lambda: body(True))
    pl.when(jnp.logical_not(has_ctx))(lambda: body(False))

    @pl.when(ki == nk - 1)
    def _():
        o_ref[...] = (acc_ref[...] / l_ref[...]).astype(BF16)


def _flash(q, k, v, *, n_batch, tb, c_len):
    m = q.shape[0]
    hd = AT_HEAD_DIM
    tq = _pick(tb, (1056, 1024, 640, 512, 256, 128))
    tk = _pick(tb, (528, 512, 640, 256, 128))
    nq, nk = tb // tq, tb // tk
    kern = functools.partial(_flash_kernel, tq=tq, tk=tk, c_len=c_len, nk=nk)
    return pl.pallas_call(
        kern, grid=(n_batch, AT_HEADS, nq, nk),
        in_specs=[pl.BlockSpec((tq, hd), lambda b, h, i, j: (b * nq + i, h)),
                  pl.BlockSpec((tk, hd), lambda b, h, i, j: (b * nk + j, h // AT_GROUP)),
                  pl.BlockSpec((tk, hd), lambda b, h, i, j: (b * nk + j, h // AT_GROUP))],
        out_specs=pl.BlockSpec((tq, hd), lambda b, h, i, j: (b * nq + i, h)),
        out_shape=jax.ShapeDtypeStruct((m, AT_WIDTH), BF16),
        scratch_shapes=[pltpu.VMEM((tq, 1), F32), pltpu.VMEM((tq, 1), F32), pltpu.VMEM((tq, hd), F32)],
        compiler_params=_params(4), name="flash_attention",
    )(q, k, v)


def _merge_kernel(yr, ys, ya, g0, g1, g2, wr, ws, wa, o_ref):
    m = _sigmoid(g0[...]) * jnp.dot(yr[...], wr[...], preferred_element_type=F32)
    m = m + _sigmoid(g1[...]) * jnp.dot(ys[...], ws[...], preferred_element_type=F32)
    m = m + _sigmoid(g2[...]) * jnp.dot(ya[...], wa[...], preferred_element_type=F32)
    o_ref[...] = m.astype(BF16)


def _merge(y_rw, y_s5, y_at, p, w_branch, *, tb):
    m = p.shape[0]
    d = w_branch.shape[1]
    tm = _pick(tb, (1056, 1024, 640, 512, 256, 128))
    tn = 512
    gcol = P_GATE // tn
    nd = d // tn
    gate = lambda br: pl.BlockSpec((tm, tn), lambda i, n: (i, gcol + br * nd + n))
    return pl.pallas_call(
        _merge_kernel, grid=(m // tm, nd),
        in_specs=[pl.BlockSpec((tm, RW_WIDTH), lambda i, n: (i, 0)),
                  pl.BlockSpec((tm, S5_WIDTH), lambda i, n: (i, 0)),
                  pl.BlockSpec((tm, AT_WIDTH), lambda i, n: (i, 0)),
                  gate(0), gate(1), gate(2),
                  pl.BlockSpec((RW_WIDTH, tn), lambda i, n: (0, n)),
                  pl.BlockSpec((S5_WIDTH, tn), lambda i, n: (1, n)),
                  pl.BlockSpec((AT_WIDTH, tn), lambda i, n: (1, n))],
        out_specs=pl.BlockSpec((tm, tn), lambda i, n: (i, n)),
        out_shape=jax.ShapeDtypeStruct((m, d), BF16),
        compiler_params=_params(2), name="merge",
    )(y_rw, y_s5, y_at, p, p, p, w_branch, w_branch, w_branch)


def _outproj_kernel(m_ref, w_ref, x_ref, mod_ref, o_ref, *, tpb, tm, c_len, n_batch):
    i = pl.program_id(0)
    is_ctx = _row_is_ctx(i, tpb, tm, c_len)
    gate = jnp.where(is_ctx, mod_ref[n_batch:n_batch + 1, :], mod_ref[pl.ds(i // tpb, 1), :])
    o_ref[...] = x_ref[...] + gate * jnp.dot(m_ref[...], w_ref[...], preferred_element_type=F32)


def _outproj(mm, w_out, x, mod, *, tb, c_len, n_batch):
    m, d = x.shape
    tm = _pick(tb, (1056, 1024, 640, 512, 256, 128))
    tn = 512
    kern = functools.partial(_outproj_kernel, tpb=tb // tm, tm=tm, c_len=c_len, n_batch=n_batch)
    return pl.pallas_call(
        kern, grid=(m // tm, d // tn),
        in_specs=[pl.BlockSpec((tm, d), lambda i, n: (i, 0)),
                  pl.BlockSpec((d, tn), lambda i, n: (0, n)),
                  pl.BlockSpec((tm, tn), lambda i, n: (i, n)),
                  pl.BlockSpec((mod.shape[0], tn), lambda i, n: (0, 2 * d // tn + n))],
        out_specs=pl.BlockSpec((tm, tn), lambda i, n: (i, n)),
        out_shape=jax.ShapeDtypeStruct((m, d), F32),
        compiler_params=_params(2), name="outproj",
    )(mm, w_out, x, mod)


def _router_kernel(x_ref, mod_ref, g_ref, rt_ref, h_ref, aff_ref, *, tpb, tm, c_len, n_batch, d):
    i = pl.program_id(0)
    is_ctx = _row_is_ctx(i, tpb, tm, c_len)
    b = i // tpb
    shift = _mod_rows(mod_ref, b, n_batch, 3, d, is_ctx)
    scale = _mod_rows(mod_ref, b, n_batch, 4, d, is_ctx)
    h = _rms(x_ref[...]) * g_ref[...]
    h = h * (1 + scale) + shift
    h_ref[...] = h
    logits = _dot_f32(rt_ref[...], h, dims=(((1,), (1,)), ((), ())))
    e = jnp.exp(logits - jnp.max(logits, axis=0, keepdims=True))
    aff_ref[...] = e / jnp.sum(e, axis=0, keepdims=True)


def _router(x, mod, g, router_t, *, tb, c_len, n_batch):
    m, d = x.shape
    ne = router_t.shape[0]
    tm = _pick(tb, (768, 640, 512, 384, 256, 128))
    kern = functools.partial(_router_kernel, tpb=tb // tm, tm=tm, c_len=c_len, n_batch=n_batch, d=d)
    return pl.pallas_call(
        kern, grid=(m // tm,),
        in_specs=[pl.BlockSpec((tm, d), lambda i: (i, 0)),
                  pl.BlockSpec(mod.shape, lambda i: (0, 0)),
                  pl.BlockSpec((1, d), lambda i: (0, 0)),
                  pl.BlockSpec((ne, d), lambda i: (0, 0))],
        out_specs=[pl.BlockSpec((tm, d), lambda i: (i, 0)), pl.BlockSpec((ne, tm), lambda i: (0, i))],
        out_shape=[jax.ShapeDtypeStruct((m, d), F32), jax.ShapeDtypeStruct((ne, m), F32)],
        compiler_params=_params(1), name="moe_router",
    )(x, mod, g.reshape(1, d), router_t)


def _select_kernel(aff_ref, tri_ref, sel_ref, *, tb, c_len, caps):
    bits = pltpu.bitcast(aff_ref[...], jnp.int32)
    ne = bits.shape[0]
    lane = lax.broadcasted_iota(jnp.int32, bits.shape, 1)
    tri = tri_ref[...]
    sel = jnp.zeros(bits.shape, F32)
    for (lo, hi), cap in zip(((0, c_len), (c_len, tb)), caps):
        vals = jnp.where((lane >= lo) & (lane < hi), bits, -1)

        def refine(it, thr, vals=vals, cap=cap):
            cand = thr | lax.shift_left(jnp.int32(1), 30 - it)
            cnt = jnp.sum(jnp.where(vals >= cand, 1.0, 0.0), axis=-1, keepdims=True)
            return jnp.where(cnt >= cap, cand, thr)

        thr = lax.fori_loop(0, 31, refine, jnp.zeros((ne, 1), jnp.int32))
        gt = vals > thr
        need = cap - jnp.sum(jnp.where(gt, 1.0, 0.0), axis=-1, keepdims=True)
        eq = jnp.where(vals == thr, 1.0, 0.0)
        run = jnp.zeros((ne, 1), F32)
        pieces = []
        for j in range(lo // 128, hi // 128):
            blk = eq[:, j * 128:(j + 1) * 128]
            pre = jnp.dot(blk.astype(BF16), tri, preferred_element_type=F32) + run
            run = run + jnp.sum(blk, axis=-1, keepdims=True)
            pieces.append(jnp.where(gt[:, j * 128:(j + 1) * 128] | ((blk > 0) & (pre <= need)), 1.0, 0.0))
        seg = jnp.concatenate(pieces, axis=1)
        if lo == 0:
            sel = jnp.concatenate([seg, sel[:, hi:]], axis=1)
        else:
            sel = jnp.concatenate([sel[:, :lo], seg], axis=1)
    sel_ref[...] = sel


def _select(aff_t, tri, *, n_batch, tb, c_len, caps):
    ne, m = aff_t.shape
    return pl.pallas_call(
        functools.partial(_select_kernel, tb=tb, c_len=c_len, caps=caps), grid=(n_batch,),
        in_specs=[pl.BlockSpec((ne, tb), lambda b: (0, b)), pl.BlockSpec(tri.shape, lambda b: (0, 0))],
        out_specs=pl.BlockSpec((ne, tb), lambda b: (0, b)),
        out_shape=jax.ShapeDtypeStruct((ne, m), F32),
        compiler_params=_params(1), name="moe_select",
    )(aff_t, tri)


def _moe_kernel(idx_ref, gate_ref, mod_ref, wg_ref, wu_ref, wd_ref, h_hbm, x_in, x_hbm, xs, acc, sem,
                *, rh, n_batch, n_split, cap_lat, d):
    del x_in
    e, b, hf = pl.program_id(0), pl.program_id(1), pl.program_id(2)
    base = ((e * n_batch + b) * n_split + hf) * rh

    def gather_h(s):
        row = idx_ref[base + s]
        return pltpu.make_async_copy(h_hbm.at[pl.ds(row, 1), :], xs.at[pl.ds(s, 1), :], sem.at[0])

    def gather_x(s):
        row = idx_ref[base + s]
        return pltpu.make_async_copy(x_hbm.at[pl.ds(row, 1), :], acc.at[pl.ds(s, 1), :], sem.at[1])

    def scatter_x(s):
        row = idx_ref[base + s]
        return pltpu.make_async_copy(acc.at[pl.ds(s, 1), :], x_hbm.at[pl.ds(row, 1), :], sem.at[2])

    def gather_start(s, c):
        gather_h(s).start()
        gather_x(s).start()
        return c

    def gather_wait(s, c):
        gather_h(s).wait()
        gather_x(s).wait()
        return c

    lax.fori_loop(0, rh, gather_start, 0)
    lax.fori_loop(0, rh, gather_wait, 0)

    xb = xs[...].astype(BF16)
    hid = _silu(jnp.dot(xb, wg_ref[0], preferred_element_type=F32)) * jnp.dot(xb, wu_ref[0], preferred_element_type=F32)
    y = jnp.dot(hid.astype(BF16), wd_ref[0], preferred_element_type=F32)
    is_ctx = hf * rh + lax.broadcasted_iota(jnp.int32, (rh, 1), 0) >= cap_lat
    g_mlp = _mod_rows(mod_ref, b, n_batch, 5, d, is_ctx)
    gate = gate_ref[0]
    for j in range(d // 128):
        sl = slice(j * 128, (j + 1) * 128)
        acc[:, sl] = acc[:, sl] + g_mlp[:, sl] * (y[:, sl] * gate)

    def scatter_start(s, c):
        scatter_x(s).start()
        return c

    def scatter_wait(s, c):
        scatter_x(s).wait()
        return c

    lax.fori_loop(0, rh, scatter_start, 0)
    lax.fori_loop(0, rh, scatter_wait, 0)


def _moe(idx, gate_b, mod, wg, wu, wd, h2, x, *, n_batch, n_split, rh, cap_lat):
    m, d = x.shape
    ne, _, f = wg.shape
    kern = functools.partial(_moe_kernel, rh=rh, n_batch=n_batch, n_split=n_split, cap_lat=cap_lat, d=d)
    grid_spec = pltpu.PrefetchScalarGridSpec(
        num_scalar_prefetch=1, grid=(ne, n_batch, n_split),
        in_specs=[pl.BlockSpec((1, rh, 128), lambda e, b, h, idx: ((e * n_batch + b) * n_split + h, 0, 0)),
                  pl.BlockSpec(mod.shape, lambda e, b, h, idx: (0, 0)),
                  pl.BlockSpec((1, d, f), lambda e, b, h, idx: (e, 0, 0)),
                  pl.BlockSpec((1, d, f), lambda e, b, h, idx: (e, 0, 0)),
                  pl.BlockSpec((1, f, d), lambda e, b, h, idx: (e, 0, 0)),
                  pl.BlockSpec(memory_space=pl.ANY),
                  pl.BlockSpec(memory_space=pl.ANY)],
        out_specs=pl.BlockSpec(memory_space=pl.ANY),
        scratch_shapes=[pltpu.VMEM((rh, d), F32), pltpu.VMEM((rh, d), F32), pltpu.SemaphoreType.DMA((3,))])
    return pl.pallas_call(
        kern, grid_spec=grid_spec, out_shape=jax.ShapeDtypeStruct((m, d), F32),
        input_output_aliases={7: 0},
        compiler_params=pltpu.CompilerParams(dimension_semantics=("arbitrary",) * 3, vmem_limit_bytes=VMEM_LIMIT,
                                             has_side_effects=True),
        name="moe_experts",
    )(idx, gate_b, mod, wg, wu, wd, h2, x)


def _compact(sel_seg, cap):
    ne, n = sel_seg.shape
    on = sel_seg > 0
    pos = jnp.where(on, jnp.cumsum(on.astype(jnp.int32), axis=1) - 1, cap)
    tok = jnp.broadcast_to(jnp.arange(n, dtype=jnp.int32)[None], (ne, n))
    out = jnp.zeros((ne, cap + 1), jnp.int32).at[jnp.arange(ne)[:, None], pos].set(tok)
    return out[:, :cap]


def _final_kernel(x_ref, g_ref, o_ref):
    o_ref[0] = _rms(x_ref[0]) * g_ref[...]


def _final_norm(x3, g, *, c_len, seq):
    n_batch, _, d = x3.shape
    tr = _pick(math.gcd(c_len, seq), (256, 128, 64, 32, 16, 8))
    off = c_len // tr
    return pl.pallas_call(
        _final_kernel, grid=(n_batch, seq // tr),
        in_specs=[pl.BlockSpec((1, tr, d), lambda b, i: (b, i + off, 0)), pl.BlockSpec((1, d), lambda b, i: (0, 0))],
        out_specs=pl.BlockSpec((1, tr, d), lambda b, i: (b, i, 0)),
        out_shape=jax.ShapeDtypeStruct((n_batch, seq, d), F32),
        compiler_params=_params(2), name="final_norm",
    )(x3, g.reshape(1, d))


def _block_ones(n, blk):
    i = jnp.arange(n)
    return (i[:, None] // blk == i[None, :] // blk).astype(BF16)


def _rope_tables(seq, c_len):
    rows = seq // GRID_W
    row = jnp.repeat(jnp.arange(rows, dtype=F32), GRID_W)
    col = jnp.tile(jnp.arange(GRID_W, dtype=F32), rows)
    quarter = AT_HEAD_DIM // 4
    freqs = ROPE_THETA ** (-jnp.arange(quarter, dtype=F32) / quarter)
    ang_r, ang_c = row[:, None] * freqs, col[:, None] * freqs
    cos = jnp.concatenate([jnp.cos(ang_r)] * 2 + [jnp.cos(ang_c)] * 2, axis=1)
    sin = jnp.concatenate([-jnp.sin(ang_r), jnp.sin(ang_r), -jnp.sin(ang_c), jnp.sin(ang_c)], axis=1)
    cos = jnp.concatenate([jnp.ones((c_len, AT_HEAD_DIM), F32), cos], axis=0)
    sin = jnp.concatenate([jnp.zeros((c_len, AT_HEAD_DIM), F32), sin], axis=0)
    return cos, sin


def kernel(x, c, ctx, c_ctx, mod_w, mod_b, norm1_g, norm2_g, w_in, rwkv_conv, rwkv_w0, rwkv_w2, rwkv_a0, rwkv_a2, rwkv_g2, rwkv_kk, rwkv_ka, rwkv_rk, rwkv_ln_g, rwkv_ln_b, s5_lam_re, s5_lam_im, s5_log_dt, s5_b_re, s5_b_im, s5_c_re, s5_c_im, s5_d, s5_glu, attn_qn, attn_kn, w_branch, w_out, router, exp_gate, exp_up, exp_down, final_g):
    n_batch, seq, d = x.shape
    c_len = ctx.shape[1]
    depth = mod_w.shape[0]
    tb = c_len + seq
    m = n_batch * tb
    tc = math.gcd(c_len, 256)
    ne = router.shape[2]
    cap_ctx, cap_lat = CAPACITY_FACTOR * c_len // ne, CAPACITY_FACTOR * seq // ne
    n_split = 2
    rh = (cap_ctx + cap_lat) // n_split

    xa = jnp.concatenate([ctx, x], axis=1).reshape(m, d)
    c_all_t = jnp.concatenate([c, c_ctx[None]], axis=0).T
    mods = _modulation(c_all_t, mod_w, mod_b)
    ones_head = _block_ones(RW_WIDTH, RW_HEAD_DIM)
    ones_pair = _block_ones(2 * RW_HEAD_DIM, RW_HEAD_DIM)
    tri = (jnp.arange(128)[:, None] <= jnp.arange(128)[None, :]).astype(BF16)
    cos_t, sin_t = _rope_tables(seq, c_len)
    zpad = jnp.zeros((d, P_S5 - OFF_S5), F32)
    row_base = (jnp.arange(n_batch, dtype=jnp.int32) * tb)[None, :, None]

    for l in range(depth):
        mod = mods[l]
        wl = w_in[l]
        w_perm = jnp.concatenate([wl[:, :OFF_S5], zpad, wl[:, OFF_S5:OFF_Q], wl[:, OFF_K:OFF_GATE],
                                  wl[:, OFF_Q:OFF_K], wl[:, OFF_GATE:]], axis=1).astype(BF16)
        p = _inproj(xa, mod, norm1_g[l], w_perm, tb=tb, c_len=c_len, n_batch=n_batch)

        r, v, kk, w0, kd0, b0, w1, kd1, b1, bonus, g = _rwkv_prep(
            p, rwkv_conv[l], rwkv_w0[l], rwkv_w2[l], rwkv_a0[l], rwkv_a2[l], rwkv_g2[l], rwkv_kk[l],
            rwkv_ka[l], rwkv_rk[l].reshape(-1), ones_head, tb=tb, c_len=c_len)
        o0, o1 = _rwkv_scan(r, v, kk, w0, kd0, b0, w1, kd1, b1, ones_pair, n_batch=n_batch, tb=tb, c_len=c_len, tc=tc)
        y_rw = _rwkv_finish(o0, o1, bonus, g, rwkv_ln_g[l], rwkv_ln_b[l], ones_head, tb=tb)

        a_rows, bres, bims, cres, cims = [], [], [], [], []
        for z in range(2):
            ab_re, ab_im, bb_re, bb_im = _s5_discretize(s5_lam_re[l, z], s5_lam_im[l, z], s5_log_dt[l, z],
                                                        s5_b_re[l, z], s5_b_im[l, z])
            a_rows += [ab_re.reshape(1, S5_NX), ab_im.reshape(1, S5_NX)]
            bres.append(_s5_pack_in(bb_re))
            bims.append(_s5_pack_in(bb_im))
            cres.append(_s5_pack_out(s5_c_re[l, z]))
            cims.append(_s5_pack_out(s5_c_im[l, z]))
        stack = lambda ts: jnp.stack(ts).astype(BF16)
        ys0, ys1 = _s5_scan(p, jnp.concatenate(a_rows, axis=0), stack(bres), stack(bims), stack(cres), stack(cims),
                            n_batch=n_batch, tb=tb, c_len=c_len, tc=tc)
        y_s5 = _s5_finish(ys0, ys1, p, s5_d[l], s5_glu[l].astype(BF16), tb=tb)

        q_r, k_r, v_b = _attn_prep(p, cos_t, sin_t, attn_qn[l], attn_kn[l], tb=tb)
        y_at = _flash(q_r, k_r, v_b, n_batch=n_batch, tb=tb, c_len=c_len)

        mm = _merge(y_rw, y_s5, y_at, p, w_branch[l].astype(BF16), tb=tb)
        x1 = _outproj(mm, w_out[l].astype(BF16), xa, mod, tb=tb, c_len=c_len, n_batch=n_batch)

        h2, aff_t = _router(x1, mod, norm2_g[l], router[l].T, tb=tb, c_len=c_len, n_batch=n_batch)
        sel = _select(aff_t, tri, n_batch=n_batch, tb=tb, c_len=c_len, caps=(cap_ctx, cap_lat))
        sel3 = sel.reshape(ne, n_batch, tb)
        idx_lat = _compact(sel3[:, :, c_len:].reshape(ne * n_batch, seq), cap_lat).reshape(ne, n_batch, cap_lat) + c_len
        idx_ctx = _compact(sel3[:, :, :c_len].reshape(ne * n_batch, c_len), cap_ctx).reshape(ne, n_batch, cap_ctx)
        idx = jnp.concatenate([idx_lat, idx_ctx], axis=2) + row_base
        gate = jnp.take_along_axis(aff_t, idx.reshape(ne, -1), axis=1)
        gate_b = jnp.broadcast_to(gate.reshape(ne * n_batch * n_split, rh, 1), (ne * n_batch * n_split, rh, 128))
        xa = _moe(idx.reshape(-1), gate_b, mod, exp_gate[l].astype(BF16), exp_up[l].astype(BF16),
                  exp_down[l].astype(BF16), h2, x1, n_batch=n_batch, n_split=n_split, rh=rh, cap_lat=cap_lat)

    return _final_norm(xa.reshape(n_batch, tb, d), final_g, c_len=c_len, seq=seq)
```

```python
import functools
import math

import jax
import jax.numpy as jnp
from jax import lax
from jax.experimental import pallas as pl
from jax.experimental.pallas import tpu as pltpu

F32 = jnp.float32
BF16 = jnp.bfloat16

NORM_EPS = 1e-6
N_MOD = 6
GRID_W = 64

RW_HEADS = 8
RW_HEAD_DIM = 64
RW_WIDTH = 512
RW_RANK = 64
RW_GATE_RANK = 128
RW_LN_EPS = 64e-5

S5_WIDTH = 512
S5_GROUP = 16
S5_GROUPS = 32
S5_STATE = 64
S5_NX = S5_GROUPS * S5_STATE
S5_PACKS = 4

AT_HEADS = 8
AT_KV_HEADS = 2
AT_GROUP = AT_HEADS // AT_KV_HEADS
AT_HEAD_DIM = 128
AT_WIDTH = 1024
AT_KV_WIDTH = 256
ROPE_THETA = 10000.0

N_EXPERTS = 16
CAPACITY_FACTOR = 2

OFF_WD = 3 * RW_WIDTH
OFF_S5 = OFF_WD + 4 * RW_RANK + RW_GATE_RANK
OFF_Q = OFF_S5 + S5_WIDTH
OFF_K = OFF_Q + AT_WIDTH
OFF_GATE = OFF_K + 2 * AT_KV_WIDTH

P_RW = 0
P_S5 = 2048
P_KV = 2560
P_Q = 3072
P_GATE = 4096

VMEM_LIMIT = 56 * 1024 * 1024


def _params(n_axes, vmem=VMEM_LIMIT):
    return pltpu.CompilerParams(dimension_semantics=("arbitrary",) * n_axes, vmem_limit_bytes=vmem)


def _pick(n, cands):
    for t in cands:
        if n % t == 0:
            return t
    raise ValueError(f"no tile for {n}")


def _split3(x):
    hi = x.astype(BF16)
    r1 = x - hi.astype(F32)
    mid = r1.astype(BF16)
    lo = (r1 - mid.astype(F32)).astype(BF16)
    return hi, mid, lo


def _dot_exact_rhs(x, w_bf16, pieces=3):
    parts = _split3(x)[:pieces]
    acc = jnp.dot(parts[0], w_bf16, preferred_element_type=F32)
    for p in parts[1:]:
        acc = acc + jnp.dot(p, w_bf16, preferred_element_type=F32)
    return acc


def _dot_f32(a, b, dims=(((1,), (0,)), ((), ()))):
    a1, a2, a3 = _split3(a)
    b1, b2, b3 = _split3(b)
    d = functools.partial(lax.dot_general, dimension_numbers=dims, preferred_element_type=F32)
    return (d(a1, b1) + (d(a1, b2) + d(a2, b1))) + ((d(a1, b3) + d(a3, b1)) + d(a2, b2))


def _sigmoid(x):
    return 1.0 / (1.0 + jnp.exp(-x))


def _silu(x):
    return x * _sigmoid(x)


def _row_is_ctx(i, tiles_per_batch, tm, c_len):
    pos = (i % tiles_per_batch) * tm + lax.broadcasted_iota(jnp.int32, (tm, 1), 0)
    return pos < c_len


def _mod_rows(mod_ref, b, n_batch, which, d, is_ctx):
    lat = mod_ref[pl.ds(b, 1), which * d:(which + 1) * d]
    ctx = mod_ref[n_batch:n_batch + 1, which * d:(which + 1) * d]
    return jnp.where(is_ctx, ctx, lat)


def _rms(x):
    return x * lax.rsqrt(jnp.mean(x * x, axis=-1, keepdims=True) + NORM_EPS)


def _mod_kernel(ct_ref, w_ref, b_ref, o_ref, *, n_rows):
    w = w_ref[0]
    for m in range(n_rows):
        col = ct_ref[:, m:m + 1]
        o_ref[0, m:m + 1, :] = jnp.sum(w * _silu(col), axis=0, keepdims=True) + b_ref[0]


def _modulation(c_all_t, mod_w, mod_b):
    depth, d, n6 = mod_w.shape
    n_rows = c_all_t.shape[1]
    tn = 512
    return pl.pallas_call(
        functools.partial(_mod_kernel, n_rows=n_rows),
        grid=(depth, n6 // tn),
        in_specs=[pl.BlockSpec((d, n_rows), lambda l, n: (0, 0)),
                  pl.BlockSpec((1, d, tn), lambda l, n: (l, 0, n)),
                  pl.BlockSpec((1, 1, tn), lambda l, n: (l, 0, n))],
        out_specs=pl.BlockSpec((1, n_rows, tn), lambda l, n: (l, 0, n)),
        out_shape=jax.ShapeDtypeStruct((depth, n_rows, n6), F32),
        compiler_params=_params(2), name="modulation",
    )(c_all_t, mod_w, mod_b.reshape(depth, 1, n6))


def _inproj_kernel(x_ref, mod_ref, g_ref, w_ref, o_ref, h_ref, *, tpb, tm, c_len, n_batch, d):
    i = pl.program_id(0)

    @pl.when(pl.program_id(1) == 0)
    def _():
        b = i // tpb
        n_chunks = next(n for n in (4, 3, 2, 1) if tm % (16 * n) == 0)
        rc = tm // n_chunks
        for j in range(n_chunks):
            rows = slice(j * rc, (j + 1) * rc)
            is_ctx = _row_is_ctx(i, tpb, tm, c_len)[rows]
            shift = _mod_rows(mod_ref, b, n_batch, 0, d, is_ctx)
            scale = _mod_rows(mod_ref, b, n_batch, 1, d, is_ctx)
            h = _rms(x_ref[rows, :]) * g_ref[...]
            h_ref[rows, :] = (h * (1 + scale) + shift).astype(BF16)

    o_ref[...] = jnp.dot(h_ref[...], w_ref[...], preferred_element_type=F32)


def _inproj(x, mod, g, w_perm, *, tb, c_len, n_batch):
    m, d = x.shape
    n_out = w_perm.shape[1]
    tm = _pick(tb, (1056, 1024, 768, 640, 512, 256, 128))
    tn = 1024
    kern = functools.partial(_inproj_kernel, tpb=tb // tm, tm=tm, c_len=c_len, n_batch=n_batch, d=d)
    return pl.pallas_call(
        kern, grid=(m // tm, n_out // tn),
        in_specs=[pl.BlockSpec((tm, d), lambda i, n: (i, 0)),
                  pl.BlockSpec(mod.shape, lambda i, n: (0, 0)),
                  pl.BlockSpec((1, d), lambda i, n: (0, 0)),
                  pl.BlockSpec((d, tn), lambda i, n: (0, n))],
        out_specs=pl.BlockSpec((tm, tn), lambda i, n: (i, n)),
        out_shape=jax.ShapeDtypeStruct((m, n_out), F32),
        scratch_shapes=[pltpu.VMEM((tm, d), BF16)],
        compiler_params=_params(2), name="inproj",
    )(x, mod, g.reshape(1, d), w_perm)


def _softplus(z):
    return jnp.maximum(z, 0.0) + jnp.log(1.0 + jnp.exp(-jnp.abs(z)))


def _rwkv_prep_kernel(p_ref, prev_ref, next_ref, conv_ref, w0_ref, w2_ref, a0_ref, a2_ref, g2_ref,
                      kk_ref, ka_ref, rk_ref, ones_ref,
                      r_o, v_o, kk_o, w0_o, kd0_o, b0_o, w1_o, kd1_o, b1_o, bonus_o, g_o,
                      *, tr, tpb, c_len, tb):
    i = pl.program_id(0)
    c = RW_WIDTH
    pos = (i % tpb) * tr + lax.broadcasted_iota(jnp.int32, (tr, 1), 0)
    row = lax.broadcasted_iota(jnp.int32, (tr, 1), 0)
    first = (pos == 0) | (pos == c_len)
    last = (pos == c_len - 1) | (pos == tb - 1)

    x = p_ref[:, 0:3 * c]
    x_prev = jnp.where(row == 0, prev_ref[7:8, 0:3 * c], pltpu.roll(x, 1, 0))
    x_next = jnp.where(row == tr - 1, next_ref[0:1, 0:3 * c], pltpu.roll(x, tr - 1, 0))
    x_prev = jnp.where(first, 0.0, x_prev)
    x_next = jnp.where(last, 0.0, x_next)
    rkv = x_prev * conv_ref[0:1, :] + x * conv_ref[1:2, :] + x_next * conv_ref[2:3, :]
    r, k, v = rkv[:, 0:c], rkv[:, c:2 * c], rkv[:, 2 * c:3 * c]
    ones = ones_ref[...]

    kk = k * kk_ref[...]
    kk = kk * lax.rsqrt(_dot_exact_rhs(kk * kk, ones) + 1e-12)
    r_o[...] = r
    v_o[...] = v
    kk_o[...] = kk

    kd_sum = jnp.zeros_like(k)
    for z, (w_o, kd_o, b_o) in enumerate(((w0_o, kd0_o, b0_o), (w1_o, kd1_o, b1_o))):
        wd = jnp.tanh(p_ref[:, OFF_WD + z * RW_RANK:OFF_WD + (z + 1) * RW_RANK])
        ad = p_ref[:, OFF_WD + (2 + z) * RW_RANK:OFF_WD + (3 + z) * RW_RANK]
        w_lin = w0_ref[z:z + 1, :] + _dot_f32(wd, w2_ref[z])
        w_log = -_softplus(-w_lin) - 0.5
        w_o[...] = -jnp.exp(w_log)
        a = _sigmoid(a0_ref[z:z + 1, :] + _dot_f32(ad, a2_ref[z]))
        kd = k * (1 + (a - 1) * ka_ref[...])
        kd_o[...] = kd
        b_o[...] = kk * a
        kd_sum = kd_sum + kd

    bonus_o[...] = _dot_exact_rhs(r * kd_sum * rk_ref[...], ones) * v
    gd = _sigmoid(p_ref[:, OFF_WD + 4 * RW_RANK:OFF_WD + 4 * RW_RANK + RW_GATE_RANK])
    g_o[...] = _dot_f32(gd, g2_ref[...])


def _rwkv_prep(p, conv_w, w0, w2, a0, a2, g2, k_k, k_a, r_k, ones_bd, *, tb, c_len):
    m = p.shape[0]
    c = RW_WIDTH
    tr = _pick(tb, (528, 512, 320, 256, 128))
    nblk8 = m // 8
    kern = functools.partial(_rwkv_prep_kernel, tr=tr, tpb=tb // tr, c_len=c_len, tb=tb)
    full = lambda a: pl.BlockSpec(a.shape, lambda i: (0,) * a.ndim)
    args = (conv_w, w0, w2, a0, a2, g2, k_k.reshape(1, c), k_a.reshape(1, c), r_k.reshape(1, c), ones_bd)
    out = jax.ShapeDtypeStruct((m, c), F32)
    return pl.pallas_call(
        kern, grid=(m // tr,),
        in_specs=[pl.BlockSpec((tr, 2048), lambda i: (i, 0)),
                  pl.BlockSpec((8, 2048), lambda i: (jnp.maximum(i * (tr // 8) - 1, 0), 0)),
                  pl.BlockSpec((8, 2048), lambda i: (jnp.minimum((i + 1) * (tr // 8), nblk8 - 1), 0))]
                 + [full(a) for a in args],
        out_specs=[pl.BlockSpec((tr, c), lambda i: (i, 0))] * 11,
        out_shape=[out] * 11,
        compiler_params=_params(1), name="rwkv_prep",
    )(p, p, p, *args)


def _rev_chunk(c, n_ctx_chunks, n_chunks):
    return jnp.where(c < n_ctx_chunks, n_ctx_chunks - 1 - c, n_chunks - 1 - (c - n_ctx_chunks))


RW_CHUNK = 64

_NN = (((1,), (0,)), ((), ()))
_NT = (((1,), (1,)), ((), ()))
_TN = (((0,), (0,)), ((), ()))


def _mm(a, b, dims=_NN, pa=1, pb=1):
    xs, ys = _split3(a)[:pa], _split3(b)[:pb]
    acc = None
    for i, x in enumerate(xs):
        for j, y in enumerate(ys):
            if i + j < max(pa, pb):
                d = lax.dot_general(x, y, dims, preferred_element_type=F32)
                acc = d if acc is None else acc + d
    return acc


_RW_PIECES = {"gram": (1, 1), "inverse": (2, 2), "values": (1, 1), "solve": (1, 1), "readout": (1, 1),
              "carry": (1, 1), "state": (2, 2)}


def _rwkv_scan_kernel(r0, v0, k0, w0, d0, b0, r1, v1, k1, w1, d1, b1, o0, o1, h_ref):
    t = RW_CHUNK
    t2 = 2 * t

    @pl.when(pl.program_id(1) == 0)
    def _():
        h_ref[...] = jnp.zeros_like(h_ref)

    r2 = lax.broadcasted_iota(jnp.int32, (t2, t2), 0)
    c2 = lax.broadcasted_iota(jnp.int32, (t2, t2), 1)
    same = (r2 // t) == (c2 // t)
    eye = r2 == c2
    row = lax.broadcasted_iota(jnp.int32, (t, t), 0)
    col = lax.broadcasted_iota(jnp.int32, (t, t), 1)
    head0 = lax.broadcasted_iota(jnp.int32, (t, t2), 1) < RW_HEAD_DIM

    def stack(x):
        return jnp.concatenate([jnp.where(head0, x, 0.0), jnp.where(head0, 0.0, x)], axis=0)

    combos = []
    for z, (r, v, kk, w, kd, bb, o) in enumerate(((r0, v0, k0, w0, d0, b0, o0), (r1, v1, k1, w1, d1, b1, o1))):
        before = (c2 % t < r2 % t) if z == 0 else (c2 % t > r2 % t)
        strict = same & before
        incl = same & (before | eye)
        cum = jnp.where((col <= row) if z == 0 else (col >= row), 1.0, 0.0)
        lw_all = w[...]
        g_all = _mm(cum, lw_all, pb=3)
        gtot_all = g_all[t - 1:t, :] if z == 0 else g_all[0:1, :]
        for hp in range(RW_WIDTH // t2):
            sl = slice(hp * t2, (hp + 1) * t2)
            g = g_all[:, sl]
            e_inv = jnp.exp(-g)
            combos.append(dict(
                z=z, hp=hp, sl=sl, o=o, strict=strict, incl=incl, gtot=jnp.exp(gtot_all[:, sl]),
                a_st=stack(kk[:, sl] * jnp.exp(g - lw_all[:, sl])), r_st=stack(r[:, sl] * jnp.exp(g)),
                bt=bb[:, sl] * e_inv, kt=kd[:, sl] * e_inv, v_st=stack(v[:, sl])))

    for c in combos:
        mm = _mm(jnp.concatenate([c["a_st"], c["r_st"]], axis=0),
                 jnp.concatenate([c["bt"], c["bt"], c["kt"], c["kt"]], axis=0), _NT, *_RW_PIECES["gram"])
        c["n_pow"] = jnp.where(c["strict"], -mm[:t2, :t2], 0.0)
        c["l_ak"] = jnp.where(c["strict"], mm[:t2, t2:], 0.0)
        c["m_rb"] = jnp.where(c["incl"], mm[t2:, :t2], 0.0)
        c["m_rk"] = jnp.where(c["incl"], mm[t2:, t2:], 0.0)
        c["inv"] = jnp.where(eye, 1.0, 0.0) + c["n_pow"]
    for c in combos:
        c["y"] = _mm(jnp.concatenate([c["l_ak"], c["m_rk"]], axis=0), c["v_st"], _NN, *_RW_PIECES["values"])
    for _ in range(5):
        for c in combos:
            c["n_pow"] = _mm(c["n_pow"], c["n_pow"], _NN, *_RW_PIECES["inverse"])
        for c in combos:
            c["inv"] = c["inv"] + _mm(c["inv"], c["n_pow"], _NN, *_RW_PIECES["inverse"])
    for c in combos:
        c["zz"] = _mm(c["inv"], jnp.concatenate([c["a_st"], c["y"][:t2]], axis=1), _NN, *_RW_PIECES["solve"])
    for c in combos:
        ro = _mm(c["m_rb"], c["zz"], _NN, *_RW_PIECES["readout"])
        c["r_hat"] = c["r_st"] - ro[:, :t2]
        c["o_hat"] = c["y"][t2:] - ro[:, t2:]
    for c in combos:
        btw = _mm(stack(c["bt"] * c["gtot"]), c["zz"], _TN, *_RW_PIECES["carry"])
        c["p"] = jnp.where(eye, c["gtot"], 0.0) - btw[:, :t2]
        c["q"] = _mm(stack(c["kt"] * c["gtot"]), c["v_st"], _TN, *_RW_PIECES["carry"]) - btw[:, t2:]
    for c in combos:
        h = h_ref[c["z"], c["hp"]]
        o_st = _mm(c["r_hat"], h, _NN, *_RW_PIECES["state"]) + c["o_hat"]
        c["o"][:, c["sl"]] = o_st[:t] + o_st[t:]
        h_ref[c["z"], c["hp"]] = _mm(c["p"], h, _NN, *_RW_PIECES["state"]) + c["q"]


def _rwkv_scan(r, v, kk, w0, kd0, b0, w1, kd1, b1, *, n_batch, tb, c_len):
    m, c = r.shape
    tc = RW_CHUNK
    nc, ncc = tb // tc, c_len // tc
    fwd = pl.BlockSpec((tc, c), lambda b, j: (b * nc + j, 0))
    rev = pl.BlockSpec((tc, c), lambda b, j: (b * nc + _rev_chunk(j, ncc, nc), 0))
    out = jax.ShapeDtypeStruct((m, c), F32)
    return pl.pallas_call(
        _rwkv_scan_kernel, grid=(n_batch, nc),
        in_specs=[fwd] * 6 + [rev] * 6,
        out_specs=[fwd, rev], out_shape=[out, out],
        scratch_shapes=[pltpu.VMEM((2, c // (2 * RW_HEAD_DIM), 2 * RW_HEAD_DIM, 2 * RW_HEAD_DIM), F32)],
        compiler_params=_params(2), name="rwkv_scan",
    )(r, v, kk, w0, kd0, b0, r, v, kk, w1, kd1, b1)


def _rwkv_finish_kernel(o0, o1, bonus, g, lng, lnb, ones_ref, y_ref):
    o = o0[...] + o1[...]
    ones = ones_ref[...]
    mu = _dot_exact_rhs(o, ones) * (1.0 / RW_HEAD_DIM)
    dev = o - mu
    var = _dot_exact_rhs(dev * dev, ones) * (1.0 / RW_HEAD_DIM)
    o_n = dev * lax.rsqrt(var + RW_LN_EPS)
    y_ref[...] = ((o_n * lng[...] + lnb[...] + bonus[...]) * g[...]).astype(BF16)


def _rwkv_finish(o0, o1, bonus, g, ln_g, ln_b, ones_bd, *, tb):
    m, c = o0.shape
    tr = _pick(tb, (1056, 1024, 640, 512, 256, 128))
    blk = pl.BlockSpec((tr, c), lambda i: (i, 0))
    vec = pl.BlockSpec((1, c), lambda i: (0, 0))
    return pl.pallas_call(
        _rwkv_finish_kernel, grid=(m // tr,),
        in_specs=[blk] * 4 + [vec, vec, pl.BlockSpec(ones_bd.shape, lambda i: (0, 0))],
        out_specs=blk, out_shape=jax.ShapeDtypeStruct((m, c), BF16),
        compiler_params=_params(1), name="rwkv_finish",
    )(o0, o1, bonus, g, ln_g.reshape(1, c), ln_b.reshape(1, c), ones_bd)


def _s5_discretize(lam_re, lam_im, log_dt, b_re, b_im):
    dt = jnp.exp(log_dt)[:, None]
    mag = jnp.exp(lam_re * dt)
    ab_re, ab_im = mag * jnp.cos(lam_im * dt), mag * jnp.sin(lam_im * dt)
    den = lam_re * lam_re + lam_im * lam_im
    num_re = ab_re - 1.0
    co_re = (num_re * lam_re + ab_im * lam_im) / den
    co_im = (ab_im * lam_re - num_re * lam_im) / den
    bb_re = co_re[..., None] * b_re - co_im[..., None] * b_im
    bb_im = co_re[..., None] * b_im + co_im[..., None] * b_re
    return ab_re, ab_im, bb_re, bb_im


def _s5_pack_in(bb):
    gpp = S5_GROUPS // S5_PACKS
    bb = bb.reshape(S5_PACKS, gpp, S5_STATE, S5_GROUP)
    eye = jnp.eye(gpp, dtype=bb.dtype)
    t = jnp.einsum('qgnc,gh->qgchn', bb, eye)
    return t.reshape(S5_PACKS, gpp * S5_GROUP, gpp * S5_STATE)


def _s5_pack_out(cc):
    gpp = S5_GROUPS // S5_PACKS
    cc = cc.reshape(S5_PACKS, gpp, S5_GROUP, S5_STATE)
    eye = jnp.eye(gpp, dtype=cc.dtype)
    t = jnp.einsum('qgcn,gh->qgnhc', cc, eye)
    return t.reshape(S5_PACKS, gpp * S5_STATE, gpp * S5_GROUP)


def _s5_scan_kernel(u0, u1, a_ref, bre, bim, cre, cim, y0, y1, xr_ref, xi_ref, st_ref, *, tc):
    @pl.when(pl.program_id(1) == 0)
    def _():
        st_ref[...] = jnp.zeros_like(st_ref)

    cw = S5_WIDTH // S5_PACKS
    nw = S5_NX // S5_PACKS
    for z, (u, y) in enumerate(((u0, y0), (u1, y1))):
        ub = u[...].astype(BF16)
        for q in range(S5_PACKS):
            uq = ub[:, q * cw:(q + 1) * cw]
            xr_ref[:, q * nw:(q + 1) * nw] = jnp.dot(uq, bre[z, q], preferred_element_type=F32)
            xi_ref[:, q * nw:(q + 1) * nw] = jnp.dot(uq, bim[z, q], preferred_element_type=F32)
        ar = a_ref[2 * z:2 * z + 1, :]
        ai = a_ref[2 * z + 1:2 * z + 2, :]

        def group(gi, carry, z=z, ar=ar, ai=ai):
            sr, si = carry
            base = pl.multiple_of((gi if z == 0 else tc // 8 - 1 - gi) * 8, 8)
            bur, bui = xr_ref[pl.ds(base, 8), :], xi_ref[pl.ds(base, 8), :]
            sub = lax.broadcasted_iota(jnp.int32, bur.shape, 0)
            xr_t, xi_t = bur, bui
            for jj in range(8):
                j = jj if z == 0 else 7 - jj
                sr, si = ar * sr - ai * si + bur[j:j + 1, :], ar * si + ai * sr + bui[j:j + 1, :]
                xr_t = jnp.where(sub == j, sr, xr_t)
                xi_t = jnp.where(sub == j, si, xi_t)
            xr_ref[pl.ds(base, 8), :] = xr_t
            xi_ref[pl.ds(base, 8), :] = xi_t
            return sr, si

        sr, si = lax.fori_loop(0, tc // 8, group, (st_ref[2 * z:2 * z + 1, :], st_ref[2 * z + 1:2 * z + 2, :]))
        st_ref[2 * z:2 * z + 1, :] = sr
        st_ref[2 * z + 1:2 * z + 2, :] = si
        xr = xr_ref[...].astype(BF16)
        xi = xi_ref[...].astype(BF16)
        for q in range(S5_PACKS):
            y[:, q * cw:(q + 1) * cw] = (
                jnp.dot(xr[:, q * nw:(q + 1) * nw], cre[z, q], preferred_element_type=F32)
                - jnp.dot(xi[:, q * nw:(q + 1) * nw], cim[z, q], preferred_element_type=F32))


def _s5_scan(p, a_all, bre, bim, cre, cim, *, n_batch, tb, c_len, tc):
    m = p.shape[0]
    nc, ncc = tb // tc, c_len // tc
    ucol = P_S5 // S5_WIDTH
    full = lambda a: pl.BlockSpec(a.shape, lambda b, j: (0,) * a.ndim)
    out = jax.ShapeDtypeStruct((m, S5_WIDTH), F32)
    return pl.pallas_call(
        functools.partial(_s5_scan_kernel, tc=tc), grid=(n_batch, nc),
        in_specs=[pl.BlockSpec((tc, S5_WIDTH), lambda b, j: (b * nc + j, ucol)),
                  pl.BlockSpec((tc, S5_WIDTH), lambda b, j: (b * nc + _rev_chunk(j, ncc, nc), ucol)),
                  full(a_all), full(bre), full(bim), full(cre), full(cim)],
        out_specs=[pl.BlockSpec((tc, S5_WIDTH), lambda b, j: (b * nc + j, 0)),
                   pl.BlockSpec((tc, S5_WIDTH), lambda b, j: (b * nc + _rev_chunk(j, ncc, nc), 0))],
        out_shape=[out, out],
        scratch_shapes=[pltpu.VMEM((tc, S5_NX), F32), pltpu.VMEM((tc, S5_NX), F32), pltpu.VMEM((4, S5_NX), F32)],
        compiler_params=_params(2), name="s5_scan",
    )(p, p, a_all, bre, bim, cre, cim)


def _s5_finish_kernel(y0, y1, u, d_ref, w_ref, o_ref):
    y = y0[...] + y1[...] + d_ref[...] * u[...]
    ge = 0.5 * y * (1.0 + jnp.tanh(math.sqrt(2.0 / math.pi) * (y + 0.044715 * (y * y * y))))
    gate = _sigmoid(jnp.dot(ge.astype(BF16), w_ref[...], preferred_element_type=F32))
    o_ref[...] = (ge * gate).astype(BF16)


def _s5_finish(y0, y1, p, d_skip, w_glu, *, tb):
    m, c = y0.shape
    tr = _pick(tb, (1056, 1024, 640, 512, 256, 128))
    blk = pl.BlockSpec((tr, c), lambda i: (i, 0))
    return pl.pallas_call(
        _s5_finish_kernel, grid=(m // tr,),
        in_specs=[blk, blk, pl.BlockSpec((tr, c), lambda i: (i, P_S5 // c)),
                  pl.BlockSpec((1, c), lambda i: (0, 0)), pl.BlockSpec((c, c), lambda i: (0, 0))],
        out_specs=blk, out_shape=jax.ShapeDtypeStruct((m, c), BF16),
        compiler_params=_params(1), name="s5_finish",
    )(y0, y1, p, d_skip.reshape(1, c), w_glu)


def _rope(x, cos, sin_signed):
    lane = lax.broadcasted_iota(jnp.int32, x.shape, 1)
    swapped = jnp.where(lane % 64 < 32, pltpu.roll(x, 96, 1), pltpu.roll(x, 32, 1))
    return x * cos + swapped * sin_signed


def _attn_prep_kernel(kv_ref, q_ref, cos_ref, sin_ref, qn_ref, kn_ref, qo, ko, vo):
    cos, sin = cos_ref[...], sin_ref[...]
    hd = AT_HEAD_DIM
    for h in range(AT_HEADS):
        q = _rms(q_ref[:, h * hd:(h + 1) * hd]) * qn_ref[...]
        qo[:, h * hd:(h + 1) * hd] = (_rope(q, cos, sin) * (hd ** -0.5)).astype(BF16)
    for h in range(AT_KV_HEADS):
        k = _rms(kv_ref[:, h * hd:(h + 1) * hd]) * kn_ref[...]
        ko[:, h * hd:(h + 1) * hd] = _rope(k, cos, sin).astype(BF16)
    vo[...] = kv_ref[:, AT_KV_WIDTH:2 * AT_KV_WIDTH].astype(BF16)


def _attn_prep(p, cos_t, sin_t, qn, kn, *, tb):
    m = p.shape[0]
    tr = _pick(tb, (528, 512, 320, 256, 128))
    tpb = tb // tr
    hd = AT_HEAD_DIM
    return pl.pallas_call(
        _attn_prep_kernel, grid=(m // tr,),
        in_specs=[pl.BlockSpec((tr, 2 * AT_KV_WIDTH), lambda i: (i, P_KV // (2 * AT_KV_WIDTH))),
                  pl.BlockSpec((tr, AT_WIDTH), lambda i: (i, P_Q // AT_WIDTH)),
                  pl.BlockSpec((tr, hd), lambda i: (i % tpb, 0)),
                  pl.BlockSpec((tr, hd), lambda i: (i % tpb, 0)),
                  pl.BlockSpec((1, hd), lambda i: (0, 0)), pl.BlockSpec((1, hd), lambda i: (0, 0))],
        out_specs=[pl.BlockSpec((tr, AT_WIDTH), lambda i: (i, 0)),
                   pl.BlockSpec((tr, AT_KV_WIDTH), lambda i: (i, 0)),
                   pl.BlockSpec((tr, AT_KV_WIDTH), lambda i: (i, 0))],
        out_shape=[jax.ShapeDtypeStruct((m, AT_WIDTH), BF16), jax.ShapeDtypeStruct((m, AT_KV_WIDTH), BF16),
                   jax.ShapeDtypeStruct((m, AT_KV_WIDTH), BF16)],
        compiler_params=_params(1), name="attn_prep",
    )(p, p, cos_t, sin_t, qn.reshape(1, hd), kn.reshape(1, hd))


def _flash_kernel(q_ref, k_ref, v_ref, o_ref, m_ref, l_ref, acc_ref, *, tq, tk, c_len, nk):
    qi, ki = pl.program_id(2), pl.program_id(3)

    @pl.when(ki == 0)
    def _():
        m_ref[...] = jnp.full_like(m_ref, -1e30)
        l_ref[...] = jnp.zeros_like(l_ref)
        acc_ref[...] = jnp.zeros_like(acc_ref)

    def body(masked):
        s = lax.dot_general(q_ref[...], k_ref[...], (((1,), (1,)), ((), ())), preferred_element_type=F32)
        if masked:
            row = qi * tq + lax.broadcasted_iota(jnp.int32, (tq, 1), 0)
            col = ki * tk + lax.broadcasted_iota(jnp.int32, (1, tk), 1)
            s = jnp.where((row < c_len) & (col >= c_len), -1e30, s)
        m_prev = m_ref[...]
        m_new = jnp.maximum(m_prev, jnp.max(s, axis=-1, keepdims=True))
        alpha = jnp.exp(m_prev - m_new)
        p = jnp.exp(s - m_new)
        l_ref[...] = alpha * l_ref[...] + jnp.sum(p, axis=-1, keepdims=True)
        acc_ref[...] = alpha * acc_ref[...] + jnp.dot(p.astype(BF16), v_ref[...], preferred_element_type=F32)
        m_ref[...] = m_new

    has_ctx = qi * tq < c_len
    pl.when(has_ctx)(lambda: body(True))
    pl.when(jnp.logical_not(has_ctx))(lambda: body(False))

    @pl.when(ki == nk - 1)
    def _():
        o_ref[...] = (acc_ref[...] / l_ref[...]).astype(BF16)


def _flash(q, k, v, *, n_batch, tb, c_len):
    m = q.shape[0]
    hd = AT_HEAD_DIM
    tq = _pick(tb, (1056, 1024, 640, 512, 256, 128))
    tk = _pick(tb, (528, 512, 640, 256, 128))
    nq, nk = tb // tq, tb // tk
    kern = functools.partial(_flash_kernel, tq=tq, tk=tk, c_len=c_len, nk=nk)
    return pl.pallas_call(
        kern, grid=(n_batch, AT_HEADS, nq, nk),
        in_specs=[pl.BlockSpec((tq, hd), lambda b, h, i, j: (b * nq + i, h)),
                  pl.BlockSpec((tk, hd), lambda b, h, i, j: (b * nk + j, h // AT_GROUP)),
                  pl.BlockSpec((tk, hd), lambda b, h, i, j: (b * nk + j, h // AT_GROUP))],
        out_specs=pl.BlockSpec((tq, hd), lambda b, h, i, j: (b * nq + i, h)),
        out_shape=jax.ShapeDtypeStruct((m, AT_WIDTH), BF16),
        scratch_shapes=[pltpu.VMEM((tq, 1), F32), pltpu.VMEM((tq, 1), F32), pltpu.VMEM((tq, hd), F32)],
        compiler_params=_params(4), name="flash_attention",
    )(q, k, v)


def _merge_kernel(yr, ys, ya, g0, g1, g2, wr, ws, wa, o_ref):
    m = _sigmoid(g0[...]) * jnp.dot(yr[...], wr[...], preferred_element_type=F32)
    m = m + _sigmoid(g1[...]) * jnp.dot(ys[...], ws[...], preferred_element_type=F32)
    m = m + _sigmoid(g2[...]) * jnp.dot(ya[...], wa[...], preferred_element_type=F32)
    o_ref[...] = m.astype(BF16)


def _merge(y_rw, y_s5, y_at, p, w_branch, *, tb):
    m = p.shape[0]
    d = w_branch.shape[1]
    tm = _pick(tb, (1056, 1024, 640, 512, 256, 128))
    tn = 512
    gcol = P_GATE // tn
    nd = d // tn
    gate = lambda br: pl.BlockSpec((tm, tn), lambda i, n: (i, gcol + br * nd + n))
    return pl.pallas_call(
        _merge_kernel, grid=(m // tm, nd),
        in_specs=[pl.BlockSpec((tm, RW_WIDTH), lambda i, n: (i, 0)),
                  pl.BlockSpec((tm, S5_WIDTH), lambda i, n: (i, 0)),
                  pl.BlockSpec((tm, AT_WIDTH), lambda i, n: (i, 0)),
                  gate(0), gate(1), gate(2),
                  pl.BlockSpec((RW_WIDTH, tn), lambda i, n: (0, n)),
                  pl.BlockSpec((S5_WIDTH, tn), lambda i, n: (1, n)),
                  pl.BlockSpec((AT_WIDTH, tn), lambda i, n: (1, n))],
        out_specs=pl.BlockSpec((tm, tn), lambda i, n: (i, n)),
        out_shape=jax.ShapeDtypeStruct((m, d), BF16),
        compiler_params=_params(2), name="merge",
    )(y_rw, y_s5, y_at, p, p, p, w_branch, w_branch, w_branch)


def _outproj_kernel(m_ref, w_ref, x_ref, mod_ref, o_ref, *, tpb, tm, c_len, n_batch):
    i = pl.program_id(0)
    is_ctx = _row_is_ctx(i, tpb, tm, c_len)
    gate = jnp.where(is_ctx, mod_ref[n_batch:n_batch + 1, :], mod_ref[pl.ds(i // tpb, 1), :])
    o_ref[...] = x_ref[...] + gate * jnp.dot(m_ref[...], w_ref[...], preferred_element_type=F32)


def _outproj(mm, w_out, x, mod, *, tb, c_len, n_batch):
    m, d = x.shape
    tm = _pick(tb, (1056, 1024, 640, 512, 256, 128))
    tn = 512
    kern = functools.partial(_outproj_kernel, tpb=tb // tm, tm=tm, c_len=c_len, n_batch=n_batch)
    return pl.pallas_call(
        kern, grid=(m // tm, d // tn),
        in_specs=[pl.BlockSpec((tm, d), lambda i, n: (i, 0)),
                  pl.BlockSpec((d, tn), lambda i, n: (0, n)),
                  pl.BlockSpec((tm, tn), lambda i, n: (i, n)),
                  pl.BlockSpec((mod.shape[0], tn), lambda i, n: (0, 2 * d // tn + n))],
        out_specs=pl.BlockSpec((tm, tn), lambda i, n: (i, n)),
        out_shape=jax.ShapeDtypeStruct((m, d), F32),
        compiler_params=_params(2), name="outproj",
    )(mm, w_out, x, mod)


def _router_kernel(x_ref, mod_ref, g_ref, rt_ref, h_ref, aff_ref, *, tpb, tm, c_len, n_batch, d):
    i = pl.program_id(0)
    is_ctx = _row_is_ctx(i, tpb, tm, c_len)
    b = i // tpb
    shift = _mod_rows(mod_ref, b, n_batch, 3, d, is_ctx)
    scale = _mod_rows(mod_ref, b, n_batch, 4, d, is_ctx)
    h = _rms(x_ref[...]) * g_ref[...]
    h = h * (1 + scale) + shift
    h_ref[...] = h
    logits = _dot_f32(rt_ref[...], h, dims=(((1,), (1,)), ((), ())))
    e = jnp.exp(logits - jnp.max(logits, axis=0, keepdims=True))
    aff_ref[...] = e / jnp.sum(e, axis=0, keepdims=True)


def _router(x, mod, g, router_t, *, tb, c_len, n_batch):
    m, d = x.shape
    ne = router_t.shape[0]
    tm = _pick(tb, (768, 640, 512, 384, 256, 128))
    kern = functools.partial(_router_kernel, tpb=tb // tm, tm=tm, c_len=c_len, n_batch=n_batch, d=d)
    return pl.pallas_call(
        kern, grid=(m // tm,),
        in_specs=[pl.BlockSpec((tm, d), lambda i: (i, 0)),
                  pl.BlockSpec(mod.shape, lambda i: (0, 0)),
                  pl.BlockSpec((1, d), lambda i: (0, 0)),
                  pl.BlockSpec((ne, d), lambda i: (0, 0))],
        out_specs=[pl.BlockSpec((tm, d), lambda i: (i, 0)), pl.BlockSpec((ne, tm), lambda i: (0, i))],
        out_shape=[jax.ShapeDtypeStruct((m, d), F32), jax.ShapeDtypeStruct((ne, m), F32)],
        compiler_params=_params(1), name="moe_router",
    )(x, mod, g.reshape(1, d), router_t)


def _select_kernel(aff_ref, tri_ref, idx_ref, csum_ref, *, tb, c_len, caps):
    bits = pltpu.bitcast(aff_ref[...], jnp.int32)
    ne = bits.shape[0]
    lane = lax.broadcasted_iota(jnp.int32, bits.shape, 1)
    tri = tri_ref[...]
    for (lo, hi), cap, out_off in zip(((0, c_len), (c_len, tb)), caps, (caps[1], 0)):
        vals = jnp.where((lane >= lo) & (lane < hi), bits, -1)

        def refine(it, thr, vals=vals, cap=cap):
            cand = thr | lax.shift_left(jnp.int32(1), 30 - it)
            cnt = jnp.sum(jnp.where(vals >= cand, 1.0, 0.0), axis=-1, keepdims=True)
            return jnp.where(cnt >= cap, cand, thr)

        thr = lax.fori_loop(0, 31, refine, jnp.zeros((ne, 1), jnp.int32))
        gt = vals > thr
        need = cap - jnp.sum(jnp.where(gt, 1.0, 0.0), axis=-1, keepdims=True)
        eq = jnp.where(vals == thr, 1.0, 0.0)
        run_eq = jnp.zeros((ne, 1), F32)
        run_sel = jnp.zeros((ne, 1), F32)
        for j in range(lo // 128, hi // 128):
            blk = eq[:, j * 128:(j + 1) * 128]
            pre = jnp.dot(blk.astype(BF16), tri, preferred_element_type=F32) + run_eq
            run_eq = run_eq + jnp.sum(blk, axis=-1, keepdims=True)
            sel = jnp.where(gt[:, j * 128:(j + 1) * 128] | ((blk > 0) & (pre <= need)), 1.0, 0.0)
            csum_ref[:, j * 128:(j + 1) * 128] = jnp.dot(sel.astype(BF16), tri, preferred_element_type=F32) + run_sel
            run_sel = run_sel + jnp.sum(sel, axis=-1, keepdims=True)

        n_seg = hi - lo
        ct = min(n_seg, 1024)
        slot = lax.broadcasted_iota(jnp.int32, (cap, 1), 0).astype(F32)
        ones = jnp.ones((ct, 128), BF16)

        def per_expert(e, carry, lo=lo, n_seg=n_seg, ct=ct, cap=cap, out_off=out_off, slot=slot, ones=ones):
            cnt = jnp.zeros((cap, 128), F32)
            for c0 in range(0, n_seg, ct):
                row = csum_ref[pl.ds(e, 1), lo + c0:lo + c0 + ct]
                cnt = cnt + jnp.dot(jnp.where(row <= slot, 1.0, 0.0).astype(BF16), ones, preferred_element_type=F32)
            idx_ref[e, 0, out_off:out_off + cap, :] = cnt.astype(jnp.int32)
            return carry

        lax.fori_loop(0, ne, per_expert, 0)


def _select(aff_t, tri, *, n_batch, tb, c_len, caps):
    ne, m = aff_t.shape
    cap_tot = caps[0] + caps[1]
    return pl.pallas_call(
        functools.partial(_select_kernel, tb=tb, c_len=c_len, caps=caps), grid=(n_batch,),
        in_specs=[pl.BlockSpec((ne, tb), lambda b: (0, b)), pl.BlockSpec(tri.shape, lambda b: (0, 0))],
        out_specs=pl.BlockSpec((ne, 1, cap_tot, 128), lambda b: (0, b, 0, 0)),
        out_shape=jax.ShapeDtypeStruct((ne, n_batch, cap_tot, 128), jnp.int32),
        scratch_shapes=[pltpu.VMEM((ne, tb), F32)],
        compiler_params=_params(1), name="moe_select",
    )(aff_t, tri)


def _moe_kernel(idx_ref, gate_ref, mod_ref, wg_ref, wu_ref, wd_ref, h_hbm, x_in, x_hbm, xs, acc, sem,
                *, rh, n_batch, n_split, cap_lat, d):
    del x_in
    e, b, hf = pl.program_id(0), pl.program_id(1), pl.program_id(2)
    base = ((e * n_batch + b) * n_split + hf) * rh

    def gather_h(s):
        row = idx_ref[base + s]
        return pltpu.make_async_copy(h_hbm.at[pl.ds(row, 1), :], xs.at[pl.ds(s, 1), :], sem.at[0])

    def gather_x(s):
        row = idx_ref[base + s]
        return pltpu.make_async_copy(x_hbm.at[pl.ds(row, 1), :], acc.at[pl.ds(s, 1), :], sem.at[1])

    def scatter_x(s):
        row = idx_ref[base + s]
        return pltpu.make_async_copy(acc.at[pl.ds(s, 1), :], x_hbm.at[pl.ds(row, 1), :], sem.at[2])

    def gather_start(s, c):
        gather_h(s).start()
        gather_x(s).start()
        return c

    def gather_wait(s, c):
        gather_h(s).wait()
        gather_x(s).wait()
        return c

    lax.fori_loop(0, rh, gather_start, 0)
    lax.fori_loop(0, rh, gather_wait, 0)

    xb = xs[...].astype(BF16)
    hid = _silu(jnp.dot(xb, wg_ref[0], preferred_element_type=F32)) * jnp.dot(xb, wu_ref[0], preferred_element_type=F32)
    y = jnp.dot(hid.astype(BF16), wd_ref[0], preferred_element_type=F32)
    is_ctx = hf * rh + lax.broadcasted_iota(jnp.int32, (rh, 1), 0) >= cap_lat
    g_mlp = _mod_rows(mod_ref, b, n_batch, 5, d, is_ctx)
    gate = gate_ref[0]
    for j in range(d // 128):
        sl = slice(j * 128, (j + 1) * 128)
        acc[:, sl] = acc[:, sl] + g_mlp[:, sl] * (y[:, sl] * gate)

    def scatter_start(s, c):
        scatter_x(s).start()
        return c

    def scatter_wait(s, c):
        scatter_x(s).wait()
        return c

    lax.fori_loop(0, rh, scatter_start, 0)
    lax.fori_loop(0, rh, scatter_wait, 0)


def _moe(idx, gate_b, mod, wg, wu, wd, h2, x, *, n_batch, n_split, rh, cap_lat):
    m, d = x.shape
    ne, _, f = wg.shape
    kern = functools.partial(_moe_kernel, rh=rh, n_batch=n_batch, n_split=n_split, cap_lat=cap_lat, d=d)
    grid_spec = pltpu.PrefetchScalarGridSpec(
        num_scalar_prefetch=1, grid=(ne, n_batch, n_split),
        in_specs=[pl.BlockSpec((1, rh, 128), lambda e, b, h, idx: ((e * n_batch + b) * n_split + h, 0, 0)),
                  pl.BlockSpec(mod.shape, lambda e, b, h, idx: (0, 0)),
                  pl.BlockSpec((1, d, f), lambda e, b, h, idx: (e, 0, 0)),
                  pl.BlockSpec((1, d, f), lambda e, b, h, idx: (e, 0, 0)),
                  pl.BlockSpec((1, f, d), lambda e, b, h, idx: (e, 0, 0)),
                  pl.BlockSpec(memory_space=pl.ANY),
                  pl.BlockSpec(memory_space=pl.ANY)],
        out_specs=pl.BlockSpec(memory_space=pl.ANY),
        scratch_shapes=[pltpu.VMEM((rh, d), F32), pltpu.VMEM((rh, d), F32), pltpu.SemaphoreType.DMA((3,))])
    return pl.pallas_call(
        kern, grid_spec=grid_spec, out_shape=jax.ShapeDtypeStruct((m, d), F32),
        input_output_aliases={7: 0},
        compiler_params=pltpu.CompilerParams(dimension_semantics=("arbitrary",) * 3, vmem_limit_bytes=VMEM_LIMIT,
                                             has_side_effects=True),
        name="moe_experts",
    )(idx, gate_b, mod, wg, wu, wd, h2, x)


def _final_kernel(x_ref, g_ref, o_ref):
    o_ref[0] = _rms(x_ref[0]) * g_ref[...]


def _final_norm(x3, g, *, c_len, seq):
    n_batch, _, d = x3.shape
    tr = _pick(math.gcd(c_len, seq), (256, 128, 64, 32, 16, 8))
    off = c_len // tr
    return pl.pallas_call(
        _final_kernel, grid=(n_batch, seq // tr),
        in_specs=[pl.BlockSpec((1, tr, d), lambda b, i: (b, i + off, 0)), pl.BlockSpec((1, d), lambda b, i: (0, 0))],
        out_specs=pl.BlockSpec((1, tr, d), lambda b, i: (b, i, 0)),
        out_shape=jax.ShapeDtypeStruct((n_batch, seq, d), F32),
        compiler_params=_params(2), name="final_norm",
    )(x3, g.reshape(1, d))


def _block_ones(n, blk):
    i = jnp.arange(n)
    return (i[:, None] // blk == i[None, :] // blk).astype(BF16)


def _rope_tables(seq, c_len):
    rows = seq // GRID_W
    row = jnp.repeat(jnp.arange(rows, dtype=F32), GRID_W)
    col = jnp.tile(jnp.arange(GRID_W, dtype=F32), rows)
    quarter = AT_HEAD_DIM // 4
    freqs = ROPE_THETA ** (-jnp.arange(quarter, dtype=F32) / quarter)
    ang_r, ang_c = row[:, None] * freqs, col[:, None] * freqs
    cos = jnp.concatenate([jnp.cos(ang_r)] * 2 + [jnp.cos(ang_c)] * 2, axis=1)
    sin = jnp.concatenate([-jnp.sin(ang_r), jnp.sin(ang_r), -jnp.sin(ang_c), jnp.sin(ang_c)], axis=1)
    cos = jnp.concatenate([jnp.ones((c_len, AT_HEAD_DIM), F32), cos], axis=0)
    sin = jnp.concatenate([jnp.zeros((c_len, AT_HEAD_DIM), F32), sin], axis=0)
    return cos, sin


def kernel(x, c, ctx, c_ctx, mod_w, mod_b, norm1_g, norm2_g, w_in, rwkv_conv, rwkv_w0, rwkv_w2, rwkv_a0, rwkv_a2, rwkv_g2, rwkv_kk, rwkv_ka, rwkv_rk, rwkv_ln_g, rwkv_ln_b, s5_lam_re, s5_lam_im, s5_log_dt, s5_b_re, s5_b_im, s5_c_re, s5_c_im, s5_d, s5_glu, attn_qn, attn_kn, w_branch, w_out, router, exp_gate, exp_up, exp_down, final_g):
    n_batch, seq, d = x.shape
    c_len = ctx.shape[1]
    depth = mod_w.shape[0]
    tb = c_len + seq
    m = n_batch * tb
    tc = math.gcd(c_len, 256)
    ne = router.shape[2]
    cap_ctx, cap_lat = CAPACITY_FACTOR * c_len // ne, CAPACITY_FACTOR * seq // ne
    n_split = 2
    rh = (cap_ctx + cap_lat) // n_split

    xa = jnp.concatenate([ctx, x], axis=1).reshape(m, d)
    c_all_t = jnp.concatenate([c, c_ctx[None]], axis=0).T
    mods = _modulation(c_all_t, mod_w, mod_b)
    ones_head = _block_ones(RW_WIDTH, RW_HEAD_DIM)
    tri = (jnp.arange(128)[:, None] <= jnp.arange(128)[None, :]).astype(BF16)
    cos_t, sin_t = _rope_tables(seq, c_len)
    zpad = jnp.zeros((d, P_S5 - OFF_S5), F32)
    row_base = (jnp.arange(n_batch, dtype=jnp.int32) * tb)[None, :, None]
    seg_base = jnp.concatenate([jnp.full((cap_lat,), c_len, jnp.int32), jnp.zeros((cap_ctx,), jnp.int32)])[None, None, :]

    for l in range(depth):
        mod = mods[l]
        wl = w_in[l]
        w_perm = jnp.concatenate([wl[:, :OFF_S5], zpad, wl[:, OFF_S5:OFF_Q], wl[:, OFF_K:OFF_GATE],
                                  wl[:, OFF_Q:OFF_K], wl[:, OFF_GATE:]], axis=1).astype(BF16)
        p = _inproj(xa, mod, norm1_g[l], w_perm, tb=tb, c_len=c_len, n_batch=n_batch)

        r, v, kk, w0, kd0, b0, w1, kd1, b1, bonus, g = _rwkv_prep(
            p, rwkv_conv[l], rwkv_w0[l], rwkv_w2[l], rwkv_a0[l], rwkv_a2[l], rwkv_g2[l], rwkv_kk[l],
            rwkv_ka[l], rwkv_rk[l].reshape(-1), ones_head, tb=tb, c_len=c_len)
        o0, o1 = _rwkv_scan(r, v, kk, w0, kd0, b0, w1, kd1, b1, n_batch=n_batch, tb=tb, c_len=c_len)
        y_rw = _rwkv_finish(o0, o1, bonus, g, rwkv_ln_g[l], rwkv_ln_b[l], ones_head, tb=tb)

        a_rows, bres, bims, cres, cims = [], [], [], [], []
        for z in range(2):
            ab_re, ab_im, bb_re, bb_im = _s5_discretize(s5_lam_re[l, z], s5_lam_im[l, z], s5_log_dt[l, z],
                                                        s5_b_re[l, z], s5_b_im[l, z])
            a_rows += [ab_re.reshape(1, S5_NX), ab_im.reshape(1, S5_NX)]
            bres.append(_s5_pack_in(bb_re))
            bims.append(_s5_pack_in(bb_im))
            cres.append(_s5_pack_out(s5_c_re[l, z]))
            cims.append(_s5_pack_out(s5_c_im[l, z]))
        stack = lambda ts: jnp.stack(ts).astype(BF16)
        ys0, ys1 = _s5_scan(p, jnp.concatenate(a_rows, axis=0), stack(bres), stack(bims), stack(cres), stack(cims),
                            n_batch=n_batch, tb=tb, c_len=c_len, tc=tc)
        y_s5 = _s5_finish(ys0, ys1, p, s5_d[l], s5_glu[l].astype(BF16), tb=tb)

        q_r, k_r, v_b = _attn_prep(p, cos_t, sin_t, attn_qn[l], attn_kn[l], tb=tb)
        y_at = _flash(q_r, k_r, v_b, n_batch=n_batch, tb=tb, c_len=c_len)

        mm = _merge(y_rw, y_s5, y_at, p, w_branch[l].astype(BF16), tb=tb)
        x1 = _outproj(mm, w_out[l].astype(BF16), xa, mod, tb=tb, c_len=c_len, n_batch=n_batch)

        h2, aff_t = _router(x1, mod, norm2_g[l], router[l].T, tb=tb, c_len=c_len, n_batch=n_batch)
        pos = _select(aff_t, tri, n_batch=n_batch, tb=tb, c_len=c_len, caps=(cap_ctx, cap_lat))[..., 0]
        idx = pos + seg_base + row_base
        gate = jnp.take_along_axis(aff_t, idx.reshape(ne, -1), axis=1)
        gate_b = jnp.broadcast_to(gate.reshape(ne * n_batch * n_split, rh, 1), (ne * n_batch * n_split, rh, 128))
        xa = _moe(idx.reshape(-1), gate_b, mod, exp_gate[l].astype(BF16), exp_up[l].astype(BF16),
                  exp_down[l].astype(BF16), h2, x1, n_batch=n_batch, n_split=n_split, rh=rh, cap_lat=cap_lat)

    return _final_norm(xa.reshape(n_batch, tb, d), final_g, c_len=c_len, seq=seq)
```

```python
import functools
import math

import jax
import jax.numpy as jnp
from jax import lax
from jax.experimental import pallas as pl
from jax.experimental.pallas import tpu as pltpu

F32 = jnp.float32
BF16 = jnp.bfloat16

NORM_EPS = 1e-6
N_MOD = 6
GRID_W = 64

RW_HEADS = 8
RW_HEAD_DIM = 64
RW_WIDTH = 512
RW_RANK = 64
RW_GATE_RANK = 128
RW_LN_EPS = 64e-5

S5_WIDTH = 512
S5_GROUP = 16
S5_GROUPS = 32
S5_STATE = 64
S5_NX = S5_GROUPS * S5_STATE
S5_PACKS = 4

AT_HEADS = 8
AT_KV_HEADS = 2
AT_GROUP = AT_HEADS // AT_KV_HEADS
AT_HEAD_DIM = 128
AT_WIDTH = 1024
AT_KV_WIDTH = 256
ROPE_THETA = 10000.0

N_EXPERTS = 16
CAPACITY_FACTOR = 2

OFF_WD = 3 * RW_WIDTH
OFF_S5 = OFF_WD + 4 * RW_RANK + RW_GATE_RANK
OFF_Q = OFF_S5 + S5_WIDTH
OFF_K = OFF_Q + AT_WIDTH
OFF_GATE = OFF_K + 2 * AT_KV_WIDTH

P_RW = 0
P_S5 = 2048
P_KV = 2560
P_Q = 3072
P_GATE = 4096

VMEM_LIMIT = 56 * 1024 * 1024


def _params(n_axes, vmem=VMEM_LIMIT):
    return pltpu.CompilerParams(dimension_semantics=("arbitrary",) * n_axes, vmem_limit_bytes=vmem)


def _pick(n, cands):
    for t in cands:
        if n % t == 0:
            return t
    raise ValueError(f"no tile for {n}")


def _split3(x):
    hi = x.astype(BF16)
    r1 = x - hi.astype(F32)
    mid = r1.astype(BF16)
    lo = (r1 - mid.astype(F32)).astype(BF16)
    return hi, mid, lo


def _dot_exact_rhs(x, w_bf16, pieces=3):
    parts = _split3(x)[:pieces]
    acc = jnp.dot(parts[0], w_bf16, preferred_element_type=F32)
    for p in parts[1:]:
        acc = acc + jnp.dot(p, w_bf16, preferred_element_type=F32)
    return acc


def _dot_f32(a, b, dims=(((1,), (0,)), ((), ()))):
    a1, a2, a3 = _split3(a)
    b1, b2, b3 = _split3(b)
    d = functools.partial(lax.dot_general, dimension_numbers=dims, preferred_element_type=F32)
    return (d(a1, b1) + (d(a1, b2) + d(a2, b1))) + ((d(a1, b3) + d(a3, b1)) + d(a2, b2))


def _sigmoid(x):
    return 1.0 / (1.0 + jnp.exp(-x))


def _silu(x):
    return x * _sigmoid(x)


def _row_is_ctx(i, tiles_per_batch, tm, c_len):
    pos = (i % tiles_per_batch) * tm + lax.broadcasted_iota(jnp.int32, (tm, 1), 0)
    return pos < c_len


def _mod_rows(mod_ref, b, n_batch, which, d, is_ctx):
    lat = mod_ref[pl.ds(b, 1), which * d:(which + 1) * d]
    ctx = mod_ref[n_batch:n_batch + 1, which * d:(which + 1) * d]
    return jnp.where(is_ctx, ctx, lat)


def _rms(x):
    return x * lax.rsqrt(jnp.mean(x * x, axis=-1, keepdims=True) + NORM_EPS)


def _mod_kernel(ct_ref, w_ref, b_ref, o_ref, *, n_rows):
    w = w_ref[0]
    for m in range(n_rows):
        col = ct_ref[:, m:m + 1]
        o_ref[0, m:m + 1, :] = jnp.sum(w * _silu(col), axis=0, keepdims=True) + b_ref[0]


def _modulation(c_all_t, mod_w, mod_b):
    depth, d, n6 = mod_w.shape
    n_rows = c_all_t.shape[1]
    tn = 512
    return pl.pallas_call(
        functools.partial(_mod_kernel, n_rows=n_rows),
        grid=(depth, n6 // tn),
        in_specs=[pl.BlockSpec((d, n_rows), lambda l, n: (0, 0)),
                  pl.BlockSpec((1, d, tn), lambda l, n: (l, 0, n)),
                  pl.BlockSpec((1, 1, tn), lambda l, n: (l, 0, n))],
        out_specs=pl.BlockSpec((1, n_rows, tn), lambda l, n: (l, 0, n)),
        out_shape=jax.ShapeDtypeStruct((depth, n_rows, n6), F32),
        compiler_params=_params(2), name="modulation",
    )(c_all_t, mod_w, mod_b.reshape(depth, 1, n6))


def _inproj_kernel(x_ref, mod_ref, g_ref, w_ref, o_ref, h_ref, *, tpb, tm, c_len, n_batch, d):
    i = pl.program_id(0)

    @pl.when(pl.program_id(1) == 0)
    def _():
        b = i // tpb
        n_chunks = next(n for n in (4, 3, 2, 1) if tm % (16 * n) == 0)
        rc = tm // n_chunks
        for j in range(n_chunks):
            rows = slice(j * rc, (j + 1) * rc)
            is_ctx = _row_is_ctx(i, tpb, tm, c_len)[rows]
            shift = _mod_rows(mod_ref, b, n_batch, 0, d, is_ctx)
            scale = _mod_rows(mod_ref, b, n_batch, 1, d, is_ctx)
            h = _rms(x_ref[rows, :]) * g_ref[...]
            h_ref[rows, :] = (h * (1 + scale) + shift).astype(BF16)

    o_ref[...] = jnp.dot(h_ref[...], w_ref[...].astype(BF16), preferred_element_type=F32)


INPROJ_TN = 512


def _w_in_offset(n):
    u = 128
    t = INPROJ_TN // u
    units = jnp.where(n < P_S5 // INPROJ_TN, n * t,
            jnp.where(n < P_KV // INPROJ_TN, OFF_S5 // u + (n - P_S5 // INPROJ_TN) * t,
            jnp.where(n < P_Q // INPROJ_TN, OFF_K // u + (n - P_KV // INPROJ_TN) * t,
            jnp.where(n < P_GATE // INPROJ_TN, OFF_Q // u + (n - P_Q // INPROJ_TN) * t,
                      OFF_GATE // u + (n - P_GATE // INPROJ_TN) * t))))
    return units * u


def _inproj(x, mod, g, w_in, *, tb, c_len, n_batch):
    m, d = x.shape
    n_out = P_GATE + w_in.shape[1] - OFF_GATE
    tm = _pick(tb, (1056, 1024, 768, 640, 512, 256, 128))
    tn = INPROJ_TN
    kern = functools.partial(_inproj_kernel, tpb=tb // tm, tm=tm, c_len=c_len, n_batch=n_batch, d=d)
    return pl.pallas_call(
        kern, grid=(m // tm, n_out // tn),
        in_specs=[pl.BlockSpec((tm, d), lambda i, n: (i, 0)),
                  pl.BlockSpec(mod.shape, lambda i, n: (0, 0)),
                  pl.BlockSpec((1, d), lambda i, n: (0, 0)),
                  pl.BlockSpec((pl.Element(d), pl.Element(tn)), lambda i, n: (0, _w_in_offset(n)))],
        out_specs=pl.BlockSpec((tm, tn), lambda i, n: (i, n)),
        out_shape=jax.ShapeDtypeStruct((m, n_out), F32),
        scratch_shapes=[pltpu.VMEM((tm, d), BF16)],
        compiler_params=_params(2), name="inproj",
    )(x, mod, g.reshape(1, d), w_in)


def _softplus(z):
    return jnp.maximum(z, 0.0) + jnp.log(1.0 + jnp.exp(-jnp.abs(z)))


def _rwkv_prep_kernel(p_ref, prev_ref, next_ref, conv_ref, w0_ref, w2_ref, a0_ref, a2_ref, g2_ref,
                      kk_ref, ka_ref, rk_ref, ones_ref,
                      r_o, v_o, kk_o, w0_o, kd0_o, b0_o, w1_o, kd1_o, b1_o, bonus_o, g_o,
                      *, tr, tpb, c_len, tb):
    i = pl.program_id(0)
    c = RW_WIDTH
    pos = (i % tpb) * tr + lax.broadcasted_iota(jnp.int32, (tr, 1), 0)
    row = lax.broadcasted_iota(jnp.int32, (tr, 1), 0)
    first = (pos == 0) | (pos == c_len)
    last = (pos == c_len - 1) | (pos == tb - 1)

    x = p_ref[:, 0:3 * c]
    x_prev = jnp.where(row == 0, prev_ref[7:8, 0:3 * c], pltpu.roll(x, 1, 0))
    x_next = jnp.where(row == tr - 1, next_ref[0:1, 0:3 * c], pltpu.roll(x, tr - 1, 0))
    x_prev = jnp.where(first, 0.0, x_prev)
    x_next = jnp.where(last, 0.0, x_next)
    rkv = x_prev * conv_ref[0:1, :] + x * conv_ref[1:2, :] + x_next * conv_ref[2:3, :]
    r, k, v = rkv[:, 0:c], rkv[:, c:2 * c], rkv[:, 2 * c:3 * c]
    ones = ones_ref[...]

    kk = k * kk_ref[...]
    kk = kk * lax.rsqrt(_dot_exact_rhs(kk * kk, ones) + 1e-12)
    r_o[...] = r
    v_o[...] = v
    kk_o[...] = kk

    kd_sum = jnp.zeros_like(k)
    for z, (w_o, kd_o, b_o) in enumerate(((w0_o, kd0_o, b0_o), (w1_o, kd1_o, b1_o))):
        wd = jnp.tanh(p_ref[:, OFF_WD + z * RW_RANK:OFF_WD + (z + 1) * RW_RANK])
        ad = p_ref[:, OFF_WD + (2 + z) * RW_RANK:OFF_WD + (3 + z) * RW_RANK]
        w_lin = w0_ref[z:z + 1, :] + _dot_f32(wd, w2_ref[z])
        w_log = -_softplus(-w_lin) - 0.5
        w_o[...] = -jnp.exp(w_log)
        a = _sigmoid(a0_ref[z:z + 1, :] + _dot_f32(ad, a2_ref[z]))
        kd = k * (1 + (a - 1) * ka_ref[...])
        kd_o[...] = kd
        b_o[...] = kk * a
        kd_sum = kd_sum + kd

    bonus_o[...] = _dot_exact_rhs(r * kd_sum * rk_ref[...], ones) * v
    gd = _sigmoid(p_ref[:, OFF_WD + 4 * RW_RANK:OFF_WD + 4 * RW_RANK + RW_GATE_RANK])
    g_o[...] = _dot_f32(gd, g2_ref[...])


def _rwkv_prep(p, conv_w, w0, w2, a0, a2, g2, k_k, k_a, r_k, ones_bd, *, tb, c_len):
    m = p.shape[0]
    c = RW_WIDTH
    tr = _pick(tb, (528, 512, 320, 256, 128))
    nblk8 = m // 8
    kern = functools.partial(_rwkv_prep_kernel, tr=tr, tpb=tb // tr, c_len=c_len, tb=tb)
    full = lambda a: pl.BlockSpec(a.shape, lambda i: (0,) * a.ndim)
    args = (conv_w, w0, w2, a0, a2, g2, k_k.reshape(1, c), k_a.reshape(1, c), r_k.reshape(1, c), ones_bd)
    out = jax.ShapeDtypeStruct((m, c), F32)
    return pl.pallas_call(
        kern, grid=(m // tr,),
        in_specs=[pl.BlockSpec((tr, 2048), lambda i: (i, 0)),
                  pl.BlockSpec((8, 2048), lambda i: (jnp.maximum(i * (tr // 8) - 1, 0), 0)),
                  pl.BlockSpec((8, 2048), lambda i: (jnp.minimum((i + 1) * (tr // 8), nblk8 - 1), 0))]
                 + [full(a) for a in args],
        out_specs=[pl.BlockSpec((tr, c), lambda i: (i, 0))] * 11,
        out_shape=[out] * 11,
        compiler_params=_params(1), name="rwkv_prep",
    )(p, p, p, *args)


def _rev_chunk(c, n_ctx_chunks, n_chunks):
    return jnp.where(c < n_ctx_chunks, n_ctx_chunks - 1 - c, n_chunks - 1 - (c - n_ctx_chunks))


RW_CHUNK = 64

_NN = (((1,), (0,)), ((), ()))
_NT = (((1,), (1,)), ((), ()))
_TN = (((0,), (0,)), ((), ()))


def _mm(a, b, dims=_NN, pa=1, pb=1):
    xs, ys = _split3(a)[:pa], _split3(b)[:pb]
    acc = None
    for i, x in enumerate(xs):
        for j, y in enumerate(ys):
            if i + j < max(pa, pb):
                d = lax.dot_general(x, y, dims, preferred_element_type=F32)
                acc = d if acc is None else acc + d
    return acc


_RW_PIECES = {"gram": (1, 1), "inverse": (2, 2), "values": (1, 1), "solve": (1, 1), "readout": (1, 1),
              "carry": (1, 1), "state": (2, 2)}


def _rwkv_scan_kernel(r0, v0, k0, w0, d0, b0, r1, v1, k1, w1, d1, b1, o0, o1, h_ref):
    t = RW_CHUNK
    t2 = 2 * t

    @pl.when(pl.program_id(1) == 0)
    def _():
        h_ref[...] = jnp.zeros_like(h_ref)

    r2 = lax.broadcasted_iota(jnp.int32, (t2, t2), 0)
    c2 = lax.broadcasted_iota(jnp.int32, (t2, t2), 1)
    same = (r2 // t) == (c2 // t)
    eye = r2 == c2
    row = lax.broadcasted_iota(jnp.int32, (t, t), 0)
    col = lax.broadcasted_iota(jnp.int32, (t, t), 1)
    head0 = lax.broadcasted_iota(jnp.int32, (t, t2), 1) < RW_HEAD_DIM

    def stack(x):
        return jnp.concatenate([jnp.where(head0, x, 0.0), jnp.where(head0, 0.0, x)], axis=0)

    combos = []
    for z, (r, v, kk, w, kd, bb, o) in enumerate(((r0, v0, k0, w0, d0, b0, o0), (r1, v1, k1, w1, d1, b1, o1))):
        before = (c2 % t < r2 % t) if z == 0 else (c2 % t > r2 % t)
        strict = same & before
        incl = same & (before | eye)
        cum = jnp.where((col <= row) if z == 0 else (col >= row), 1.0, 0.0)
        lw_all = w[...]
        g_all = _mm(cum, lw_all, pb=3)
        gtot_all = g_all[t - 1:t, :] if z == 0 else g_all[0:1, :]
        for hp in range(RW_WIDTH // t2):
            sl = slice(hp * t2, (hp + 1) * t2)
            g = g_all[:, sl]
            e_inv = jnp.exp(-g)
            combos.append(dict(
                z=z, hp=hp, sl=sl, o=o, strict=strict, incl=incl, gtot=jnp.exp(gtot_all[:, sl]),
                a_st=stack(kk[:, sl] * jnp.exp(g - lw_all[:, sl])), r_st=stack(r[:, sl] * jnp.exp(g)),
                bt=bb[:, sl] * e_inv, kt=kd[:, sl] * e_inv, v_st=stack(v[:, sl])))

    for c in combos:
        mm = _mm(jnp.concatenate([c["a_st"], c["r_st"]], axis=0),
                 jnp.concatenate([c["bt"], c["bt"], c["kt"], c["kt"]], axis=0), _NT, *_RW_PIECES["gram"])
        c["n_pow"] = jnp.where(c["strict"], -mm[:t2, :t2], 0.0)
        c["l_ak"] = jnp.where(c["strict"], mm[:t2, t2:], 0.0)
        c["m_rb"] = jnp.where(c["incl"], mm[t2:, :t2], 0.0)
        c["m_rk"] = jnp.where(c["incl"], mm[t2:, t2:], 0.0)
        c["inv"] = jnp.where(eye, 1.0, 0.0) + c["n_pow"]
    for c in combos:
        c["y"] = _mm(jnp.concatenate([c["l_ak"], c["m_rk"]], axis=0), c["v_st"], _NN, *_RW_PIECES["values"])
    for _ in range(5):
        for c in combos:
            c["n_pow"] = _mm(c["n_pow"], c["n_pow"], _NN, *_RW_PIECES["inverse"])
        for c in combos:
            c["inv"] = c["inv"] + _mm(c["inv"], c["n_pow"], _NN, *_RW_PIECES["inverse"])
    for c in combos:
        c["zz"] = _mm(c["inv"], jnp.concatenate([c["a_st"], c["y"][:t2]], axis=1), _NN, *_RW_PIECES["solve"])
    for c in combos:
        ro = _mm(c["m_rb"], c["zz"], _NN, *_RW_PIECES["readout"])
        c["r_hat"] = c["r_st"] - ro[:, :t2]
        c["o_hat"] = c["y"][t2:] - ro[:, t2:]
    for c in combos:
        btw = _mm(stack(c["bt"] * c["gtot"]), c["zz"], _TN, *_RW_PIECES["carry"])
        c["p"] = jnp.where(eye, c["gtot"], 0.0) - btw[:, :t2]
        c["q"] = _mm(stack(c["kt"] * c["gtot"]), c["v_st"], _TN, *_RW_PIECES["carry"]) - btw[:, t2:]
    for c in combos:
        h = h_ref[c["z"], c["hp"]]
        o_st = _mm(c["r_hat"], h, _NN, *_RW_PIECES["state"]) + c["o_hat"]
        c["o"][:, c["sl"]] = o_st[:t] + o_st[t:]
        h_ref[c["z"], c["hp"]] = _mm(c["p"], h, _NN, *_RW_PIECES["state"]) + c["q"]


def _rwkv_scan(r, v, kk, w0, kd0, b0, w1, kd1, b1, *, n_batch, tb, c_len):
    m, c = r.shape
    tc = RW_CHUNK
    nc, ncc = tb // tc, c_len // tc
    fwd = pl.BlockSpec((tc, c), lambda b, j: (b * nc + j, 0))
    rev = pl.BlockSpec((tc, c), lambda b, j: (b * nc + _rev_chunk(j, ncc, nc), 0))
    out = jax.ShapeDtypeStruct((m, c), F32)
    return pl.pallas_call(
        _rwkv_scan_kernel, grid=(n_batch, nc),
        in_specs=[fwd] * 6 + [rev] * 6,
        out_specs=[fwd, rev], out_shape=[out, out],
        scratch_shapes=[pltpu.VMEM((2, c // (2 * RW_HEAD_DIM), 2 * RW_HEAD_DIM, 2 * RW_HEAD_DIM), F32)],
        compiler_params=_params(2), name="rwkv_scan",
    )(r, v, kk, w0, kd0, b0, r, v, kk, w1, kd1, b1)


def _rwkv_finish_kernel(o0, o1, bonus, g, lng, lnb, ones_ref, y_ref):
    o = o0[...] + o1[...]
    ones = ones_ref[...]
    mu = _dot_exact_rhs(o, ones) * (1.0 / RW_HEAD_DIM)
    dev = o - mu
    var = _dot_exact_rhs(dev * dev, ones) * (1.0 / RW_HEAD_DIM)
    o_n = dev * lax.rsqrt(var + RW_LN_EPS)
    y_ref[...] = ((o_n * lng[...] + lnb[...] + bonus[...]) * g[...]).astype(BF16)


def _rwkv_finish(o0, o1, bonus, g, ln_g, ln_b, ones_bd, *, tb):
    m, c = o0.shape
    tr = _pick(tb, (1056, 1024, 640, 512, 256, 128))
    blk = pl.BlockSpec((tr, c), lambda i: (i, 0))
    vec = pl.BlockSpec((1, c), lambda i: (0, 0))
    return pl.pallas_call(
        _rwkv_finish_kernel, grid=(m // tr,),
        in_specs=[blk] * 4 + [vec, vec, pl.BlockSpec(ones_bd.shape, lambda i: (0, 0))],
        out_specs=blk, out_shape=jax.ShapeDtypeStruct((m, c), BF16),
        compiler_params=_params(1), name="rwkv_finish",
    )(o0, o1, bonus, g, ln_g.reshape(1, c), ln_b.reshape(1, c), ones_bd)


def _s5_discretize(lam_re, lam_im, log_dt, b_re, b_im):
    dt = jnp.exp(log_dt)[:, None]
    mag = jnp.exp(lam_re * dt)
    ab_re, ab_im = mag * jnp.cos(lam_im * dt), mag * jnp.sin(lam_im * dt)
    den = lam_re * lam_re + lam_im * lam_im
    num_re = ab_re - 1.0
    co_re = (num_re * lam_re + ab_im * lam_im) / den
    co_im = (ab_im * lam_re - num_re * lam_im) / den
    bb_re = co_re[..., None] * b_re - co_im[..., None] * b_im
    bb_im = co_re[..., None] * b_im + co_im[..., None] * b_re
    return ab_re, ab_im, bb_re, bb_im


def _s5_pack_in(bb):
    gpp = S5_GROUPS // S5_PACKS
    bb = bb.reshape(S5_PACKS, gpp, S5_STATE, S5_GROUP)
    eye = jnp.eye(gpp, dtype=bb.dtype)
    t = jnp.einsum('qgnc,gh->qgchn', bb, eye)
    return t.reshape(S5_PACKS, gpp * S5_GROUP, gpp * S5_STATE)


def _s5_pack_out(cc):
    gpp = S5_GROUPS // S5_PACKS
    cc = cc.reshape(S5_PACKS, gpp, S5_GROUP, S5_STATE)
    eye = jnp.eye(gpp, dtype=cc.dtype)
    t = jnp.einsum('qgcn,gh->qgnhc', cc, eye)
    return t.reshape(S5_PACKS, gpp * S5_STATE, gpp * S5_GROUP)


def _s5_scan_kernel(u0, u1, a_ref, bre, bim, cre, cim, y0, y1, xr_ref, xi_ref, st_ref, *, tc):
    @pl.when(pl.program_id(1) == 0)
    def _():
        st_ref[...] = jnp.zeros_like(st_ref)

    cw = S5_WIDTH // S5_PACKS
    nw = S5_NX // S5_PACKS
    for z, (u, y) in enumerate(((u0, y0), (u1, y1))):
        ub = u[...].astype(BF16)
        for q in range(S5_PACKS):
            uq = ub[:, q * cw:(q + 1) * cw]
            xr_ref[:, q * nw:(q + 1) * nw] = jnp.dot(uq, bre[z, q], preferred_element_type=F32)
            xi_ref[:, q * nw:(q + 1) * nw] = jnp.dot(uq, bim[z, q], preferred_element_type=F32)
        ar = a_ref[2 * z:2 * z + 1, :]
        ai = a_ref[2 * z + 1:2 * z + 2, :]

        def group(gi, carry, z=z, ar=ar, ai=ai):
            sr, si = carry
            base = pl.multiple_of((gi if z == 0 else tc // 8 - 1 - gi) * 8, 8)
            bur, bui = xr_ref[pl.ds(base, 8), :], xi_ref[pl.ds(base, 8), :]
            sub = lax.broadcasted_iota(jnp.int32, bur.shape, 0)
            xr_t, xi_t = bur, bui
            for jj in range(8):
                j = jj if z == 0 else 7 - jj
                sr, si = ar * sr - ai * si + bur[j:j + 1, :], ar * si + ai * sr + bui[j:j + 1, :]
                xr_t = jnp.where(sub == j, sr, xr_t)
                xi_t = jnp.where(sub == j, si, xi_t)
            xr_ref[pl.ds(base, 8), :] = xr_t
            xi_ref[pl.ds(base, 8), :] = xi_t
            return sr, si

        sr, si = lax.fori_loop(0, tc // 8, group, (st_ref[2 * z:2 * z + 1, :], st_ref[2 * z + 1:2 * z + 2, :]))
        st_ref[2 * z:2 * z + 1, :] = sr
        st_ref[2 * z + 1:2 * z + 2, :] = si
        xr = xr_ref[...].astype(BF16)
        xi = xi_ref[...].astype(BF16)
        for q in range(S5_PACKS):
            y[:, q * cw:(q + 1) * cw] = (
                jnp.dot(xr[:, q * nw:(q + 1) * nw], cre[z, q], preferred_element_type=F32)
                - jnp.dot(xi[:, q * nw:(q + 1) * nw], cim[z, q], preferred_element_type=F32))


def _s5_scan(p, a_all, bre, bim, cre, cim, *, n_batch, tb, c_len, tc):
    m = p.shape[0]
    nc, ncc = tb // tc, c_len // tc
    ucol = P_S5 // S5_WIDTH
    full = lambda a: pl.BlockSpec(a.shape, lambda b, j: (0,) * a.ndim)
    out = jax.ShapeDtypeStruct((m, S5_WIDTH), F32)
    return pl.pallas_call(
        functools.partial(_s5_scan_kernel, tc=tc), grid=(n_batch, nc),
        in_specs=[pl.BlockSpec((tc, S5_WIDTH), lambda b, j: (b * nc + j, ucol)),
                  pl.BlockSpec((tc, S5_WIDTH), lambda b, j: (b * nc + _rev_chunk(j, ncc, nc), ucol)),
                  full(a_all), full(bre), full(bim), full(cre), full(cim)],
        out_specs=[pl.BlockSpec((tc, S5_WIDTH), lambda b, j: (b * nc + j, 0)),
                   pl.BlockSpec((tc, S5_WIDTH), lambda b, j: (b * nc + _rev_chunk(j, ncc, nc), 0))],
        out_shape=[out, out],
        scratch_shapes=[pltpu.VMEM((tc, S5_NX), F32), pltpu.VMEM((tc, S5_NX), F32), pltpu.VMEM((4, S5_NX), F32)],
        compiler_params=_params(2), name="s5_scan",
    )(p, p, a_all, bre, bim, cre, cim)


def _s5_finish_kernel(y0, y1, u, d_ref, w_ref, o_ref):
    y = y0[...] + y1[...] + d_ref[...] * u[...]
    ge = 0.5 * y * (1.0 + jnp.tanh(math.sqrt(2.0 / math.pi) * (y + 0.044715 * (y * y * y))))
    gate = _sigmoid(jnp.dot(ge.astype(BF16), w_ref[...], preferred_element_type=F32))
    o_ref[...] = (ge * gate).astype(BF16)


def _s5_finish(y0, y1, p, d_skip, w_glu, *, tb):
    m, c = y0.shape
    tr = _pick(tb, (1056, 1024, 640, 512, 256, 128))
    blk = pl.BlockSpec((tr, c), lambda i: (i, 0))
    return pl.pallas_call(
        _s5_finish_kernel, grid=(m // tr,),
        in_specs=[blk, blk, pl.BlockSpec((tr, c), lambda i: (i, P_S5 // c)),
                  pl.BlockSpec((1, c), lambda i: (0, 0)), pl.BlockSpec((c, c), lambda i: (0, 0))],
        out_specs=blk, out_shape=jax.ShapeDtypeStruct((m, c), BF16),
        compiler_params=_params(1), name="s5_finish",
    )(y0, y1, p, d_skip.reshape(1, c), w_glu)


def _rope(x, cos, sin_signed):
    lane = lax.broadcasted_iota(jnp.int32, x.shape, 1)
    swapped = jnp.where(lane % 64 < 32, pltpu.roll(x, 96, 1), pltpu.roll(x, 32, 1))
    return x * cos + swapped * sin_signed


def _attn_prep_kernel(kv_ref, q_ref, cos_ref, sin_ref, qn_ref, kn_ref, qo, ko, vo):
    cos, sin = cos_ref[...], sin_ref[...]
    hd = AT_HEAD_DIM
    for h in range(AT_HEADS):
        q = _rms(q_ref[:, h * hd:(h + 1) * hd]) * qn_ref[...]
        qo[:, h * hd:(h + 1) * hd] = (_rope(q, cos, sin) * (hd ** -0.5 * math.log2(math.e))).astype(BF16)
    for h in range(AT_KV_HEADS):
        k = _rms(kv_ref[:, h * hd:(h + 1) * hd]) * kn_ref[...]
        ko[:, h * hd:(h + 1) * hd] = _rope(k, cos, sin).astype(BF16)
    vo[...] = kv_ref[:, AT_KV_WIDTH:2 * AT_KV_WIDTH].astype(BF16)


def _attn_prep(p, cos_t, sin_t, qn, kn, *, tb):
    m = p.shape[0]
    tr = _pick(tb, (528, 512, 320, 256, 128))
    tpb = tb // tr
    hd = AT_HEAD_DIM
    return pl.pallas_call(
        _attn_prep_kernel, grid=(m // tr,),
        in_specs=[pl.BlockSpec((tr, 2 * AT_KV_WIDTH), lambda i: (i, P_KV // (2 * AT_KV_WIDTH))),
                  pl.BlockSpec((tr, AT_WIDTH), lambda i: (i, P_Q // AT_WIDTH)),
                  pl.BlockSpec((tr, hd), lambda i: (i % tpb, 0)),
                  pl.BlockSpec((tr, hd), lambda i: (i % tpb, 0)),
                  pl.BlockSpec((1, hd), lambda i: (0, 0)), pl.BlockSpec((1, hd), lambda i: (0, 0))],
        out_specs=[pl.BlockSpec((tr, AT_WIDTH), lambda i: (i, 0)),
                   pl.BlockSpec((tr, AT_KV_WIDTH), lambda i: (i, 0)),
                   pl.BlockSpec((tr, AT_KV_WIDTH), lambda i: (i, 0))],
        out_shape=[jax.ShapeDtypeStruct((m, AT_WIDTH), BF16), jax.ShapeDtypeStruct((m, AT_KV_WIDTH), BF16),
                   jax.ShapeDtypeStruct((m, AT_KV_WIDTH), BF16)],
        compiler_params=_params(1), name="attn_prep",
    )(p, p, cos_t, sin_t, qn.reshape(1, hd), kn.reshape(1, hd))


FLASH_TQ = (1056, 1024, 640, 512, 256, 128)
FLASH_TK = (768, 640, 512, 384, 256, 128)


def _flash_kernel(q_ref, k_ref, v_ref, o_ref, m_ref, acc_ref, s_ref, p_ref, *, tq, tk, c_len, nk):
    qi, ki = pl.program_id(2), pl.program_id(3)
    hd = AT_HEAD_DIM

    @pl.when(ki == 0)
    def _():
        m_ref[...] = jnp.full_like(m_ref, -1e30)
        acc_ref[...] = jnp.zeros_like(acc_ref)

    def body(masked):
        k = k_ref[...]
        v_ext = jnp.concatenate([v_ref[...], jnp.ones((tk, hd), BF16)], axis=1)
        if masked:
            row = qi * tq + lax.broadcasted_iota(jnp.int32, (tq, 1), 0)
            col = ki * tk + lax.broadcasted_iota(jnp.int32, (1, tk), 1)
            hide = (row < c_len) & (col >= c_len)
        for g in range(AT_GROUP):
            b = g % 2
            s = lax.dot_general(q_ref[:, g * hd:(g + 1) * hd], k, _NT, preferred_element_type=F32)
            s_ref[b] = jnp.where(hide, -1e30, s) if masked else s
            m_prev = m_ref[g]
            m_new = jnp.maximum(m_prev, jnp.max(s_ref[b], axis=-1, keepdims=True))
            m_ref[g] = m_new
            p_ref[b] = jnp.exp2(s_ref[b] - m_new).astype(BF16)
            acc_ref[g] = jnp.exp2(m_prev - m_new) * acc_ref[g] + jnp.dot(p_ref[b], v_ext, preferred_element_type=F32)

    has_ctx = qi * tq < c_len
    pl.when(has_ctx)(lambda: body(True))
    pl.when(jnp.logical_not(has_ctx))(lambda: body(False))

    @pl.when(ki == nk - 1)
    def _():
        for g in range(AT_GROUP):
            o_ref[:, g * hd:(g + 1) * hd] = (acc_ref[g, :, :hd] / acc_ref[g, :, hd:]).astype(BF16)


def _flash(q, k, v, *, n_batch, tb, c_len):
    m = q.shape[0]
    hd = AT_HEAD_DIM
    gw = AT_GROUP * hd
    tq = _pick(tb, FLASH_TQ)
    tk = _pick(tb, FLASH_TK)
    nq, nk = tb // tq, tb // tk
    kern = functools.partial(_flash_kernel, tq=tq, tk=tk, c_len=c_len, nk=nk)
    return pl.pallas_call(
        kern, grid=(n_batch, AT_KV_HEADS, nq, nk),
        in_specs=[pl.BlockSpec((tq, gw), lambda b, h, i, j: (b * nq + i, h)),
                  pl.BlockSpec((tk, hd), lambda b, h, i, j: (b * nk + j, h)),
                  pl.BlockSpec((tk, hd), lambda b, h, i, j: (b * nk + j, h))],
        out_specs=pl.BlockSpec((tq, gw), lambda b, h, i, j: (b * nq + i, h)),
        out_shape=jax.ShapeDtypeStruct((m, AT_WIDTH), BF16),
        scratch_shapes=[pltpu.VMEM((AT_GROUP, tq, 1), F32), pltpu.VMEM((AT_GROUP, tq, 2 * hd), F32),
                        pltpu.VMEM((2, tq, tk), F32), pltpu.VMEM((2, tq, tk), BF16)],
        compiler_params=_params(4), name="flash_attention",
    )(q, k, v)


def _merge_kernel(yr, ys, ya, g0, g1, g2, wr, ws, wa, o_ref):
    m = _sigmoid(g0[...]) * jnp.dot(yr[...], wr[...], preferred_element_type=F32)
    m = m + _sigmoid(g1[...]) * jnp.dot(ys[...], ws[...], preferred_element_type=F32)
    m = m + _sigmoid(g2[...]) * jnp.dot(ya[...], wa[...], preferred_element_type=F32)
    o_ref[...] = m.astype(BF16)


def _merge(y_rw, y_s5, y_at, p, w_branch, *, tb):
    m = p.shape[0]
    d = w_branch.shape[1]
    tm = _pick(tb, (1056, 1024, 640, 512, 256, 128))
    tn = 512
    gcol = P_GATE // tn
    nd = d // tn
    gate = lambda br: pl.BlockSpec((tm, tn), lambda i, n: (i, gcol + br * nd + n))
    return pl.pallas_call(
        _merge_kernel, grid=(m // tm, nd),
        in_specs=[pl.BlockSpec((tm, RW_WIDTH), lambda i, n: (i, 0)),
                  pl.BlockSpec((tm, S5_WIDTH), lambda i, n: (i, 0)),
                  pl.BlockSpec((tm, AT_WIDTH), lambda i, n: (i, 0)),
                  gate(0), gate(1), gate(2),
                  pl.BlockSpec((RW_WIDTH, tn), lambda i, n: (0, n)),
                  pl.BlockSpec((S5_WIDTH, tn), lambda i, n: (1, n)),
                  pl.BlockSpec((AT_WIDTH, tn), lambda i, n: (1, n))],
        out_specs=pl.BlockSpec((tm, tn), lambda i, n: (i, n)),
        out_shape=jax.ShapeDtypeStruct((m, d), BF16),
        compiler_params=_params(2), name="merge",
    )(y_rw, y_s5, y_at, p, p, p, w_branch, w_branch, w_branch)


def _outproj_kernel(m_ref, w_ref, x_ref, mod_ref, o_ref, *, tpb, tm, c_len, n_batch):
    i = pl.program_id(0)
    is_ctx = _row_is_ctx(i, tpb, tm, c_len)
    gate = jnp.where(is_ctx, mod_ref[n_batch:n_batch + 1, :], mod_ref[pl.ds(i // tpb, 1), :])
    o_ref[...] = x_ref[...] + gate * jnp.dot(m_ref[...], w_ref[...], preferred_element_type=F32)


def _outproj(mm, w_out, x, mod, *, tb, c_len, n_batch):
    m, d = x.shape
    tm = _pick(tb, (1056, 1024, 640, 512, 256, 128))
    tn = 512
    kern = functools.partial(_outproj_kernel, tpb=tb // tm, tm=tm, c_len=c_len, n_batch=n_batch)
    return pl.pallas_call(
        kern, grid=(m // tm, d // tn),
        in_specs=[pl.BlockSpec((tm, d), lambda i, n: (i, 0)),
                  pl.BlockSpec((d, tn), lambda i, n: (0, n)),
                  pl.BlockSpec((tm, tn), lambda i, n: (i, n)),
                  pl.BlockSpec((mod.shape[0], tn), lambda i, n: (0, 2 * d // tn + n))],
        out_specs=pl.BlockSpec((tm, tn), lambda i, n: (i, n)),
        out_shape=jax.ShapeDtypeStruct((m, d), F32),
        compiler_params=_params(2), name="outproj",
    )(mm, w_out, x, mod)


def _router_kernel(x_ref, mod_ref, g_ref, rt_ref, h_ref, aff_ref, *, tpb, tm, c_len, n_batch, d):
    i = pl.program_id(0)
    is_ctx = _row_is_ctx(i, tpb, tm, c_len)
    b = i // tpb
    shift = _mod_rows(mod_ref, b, n_batch, 3, d, is_ctx)
    scale = _mod_rows(mod_ref, b, n_batch, 4, d, is_ctx)
    h = _rms(x_ref[...]) * g_ref[...]
    h = h * (1 + scale) + shift
    h_ref[...] = h
    logits = _dot_f32(rt_ref[...], h, dims=(((1,), (1,)), ((), ())))
    e = jnp.exp(logits - jnp.max(logits, axis=0, keepdims=True))
    aff_ref[...] = e / jnp.sum(e, axis=0, keepdims=True)


def _router(x, mod, g, router_t, *, tb, c_len, n_batch):
    m, d = x.shape
    ne = router_t.shape[0]
    tm = _pick(tb, (768, 640, 512, 384, 256, 128))
    kern = functools.partial(_router_kernel, tpb=tb // tm, tm=tm, c_len=c_len, n_batch=n_batch, d=d)
    return pl.pallas_call(
        kern, grid=(m // tm,),
        in_specs=[pl.BlockSpec((tm, d), lambda i: (i, 0)),
                  pl.BlockSpec(mod.shape, lambda i: (0, 0)),
                  pl.BlockSpec((1, d), lambda i: (0, 0)),
                  pl.BlockSpec((ne, d), lambda i: (0, 0))],
        out_specs=[pl.BlockSpec((tm, d), lambda i: (i, 0)), pl.BlockSpec((ne, tm), lambda i: (0, i))],
        out_shape=[jax.ShapeDtypeStruct((m, d), F32), jax.ShapeDtypeStruct((ne, m), F32)],
        compiler_params=_params(1), name="moe_router",
    )(x, mod, g.reshape(1, d), router_t)


def _select_kernel(aff_ref, tri_ref, idx_ref, csum_ref, *, tb, c_len, caps):
    bits = pltpu.bitcast(aff_ref[...], jnp.int32)
    ne = bits.shape[0]
    lane = lax.broadcasted_iota(jnp.int32, bits.shape, 1)
    tri = tri_ref[...]
    for (lo, hi), cap, out_off in zip(((0, c_len), (c_len, tb)), caps, (caps[1], 0)):
        vals = jnp.where((lane >= lo) & (lane < hi), bits, -1)

        def refine(it, thr, vals=vals, cap=cap):
            cand = thr | lax.shift_left(jnp.int32(1), 30 - it)
            cnt = jnp.sum(jnp.where(vals >= cand, 1.0, 0.0), axis=-1, keepdims=True)
            return jnp.where(cnt >= cap, cand, thr)

        thr = lax.fori_loop(0, 31, refine, jnp.zeros((ne, 1), jnp.int32))
        gt = vals > thr
        need = cap - jnp.sum(jnp.where(gt, 1.0, 0.0), axis=-1, keepdims=True)
        eq = jnp.where(vals == thr, 1.0, 0.0)
        run_eq = jnp.zeros((ne, 1), F32)
        run_sel = jnp.zeros((ne, 1), F32)
        for j in range(lo // 128, hi // 128):
            blk = eq[:, j * 128:(j + 1) * 128]
            pre = jnp.dot(blk.astype(BF16), tri, preferred_element_type=F32) + run_eq
            run_eq = run_eq + jnp.sum(blk, axis=-1, keepdims=True)
            sel = jnp.where(gt[:, j * 128:(j + 1) * 128] | ((blk > 0) & (pre <= need)), 1.0, 0.0)
            csum_ref[:, j * 128:(j + 1) * 128] = jnp.dot(sel.astype(BF16), tri, preferred_element_type=F32) + run_sel
            run_sel = run_sel + jnp.sum(sel, axis=-1, keepdims=True)

        n_seg = hi - lo
        ct = min(n_seg, 1024)
        slot = lax.broadcasted_iota(jnp.int32, (cap, 1), 0).astype(F32)
        ones = jnp.ones((ct, 128), BF16)

        def per_expert(e, carry, lo=lo, n_seg=n_seg, ct=ct, cap=cap, out_off=out_off, slot=slot, ones=ones):
            cnt = jnp.zeros((cap, 128), F32)
            for c0 in range(0, n_seg, ct):
                row = csum_ref[pl.ds(e, 1), lo + c0:lo + c0 + ct]
                cnt = cnt + jnp.dot(jnp.where(row <= slot, 1.0, 0.0).astype(BF16), ones, preferred_element_type=F32)
            idx_ref[e, 0, out_off:out_off + cap, :] = cnt.astype(jnp.int32)
            return carry

        lax.fori_loop(0, ne, per_expert, 0)


def _select(aff_t, tri, *, n_batch, tb, c_len, caps):
    ne, m = aff_t.shape
    cap_tot = caps[0] + caps[1]
    return pl.pallas_call(
        functools.partial(_select_kernel, tb=tb, c_len=c_len, caps=caps), grid=(n_batch,),
        in_specs=[pl.BlockSpec((ne, tb), lambda b: (0, b)), pl.BlockSpec(tri.shape, lambda b: (0, 0))],
        out_specs=pl.BlockSpec((ne, 1, cap_tot, 128), lambda b: (0, b, 0, 0)),
        out_shape=jax.ShapeDtypeStruct((ne, n_batch, cap_tot, 128), jnp.int32),
        scratch_shapes=[pltpu.VMEM((ne, tb), F32)],
        compiler_params=_params(1), name="moe_select",
    )(aff_t, tri)


MOE_FF_CHUNK = 256
MOE_DMA_UNROLL = 8


def _moe_kernel(idx_ref, gate_ref, mod_ref, wg_ref, wu_ref, wd_ref, h_hbm, x_in, x_hbm, ys, xb, xrow, sem,
                *, rows, n_batch, cap_lat, d, n_ff):
    del x_in
    e, b, f = pl.program_id(0), pl.program_id(1), pl.program_id(2)
    base = (e * n_batch + b) * rows

    def gather_h(s):
        return pltpu.make_async_copy(h_hbm.at[pl.ds(idx_ref[base + s], 1), :], ys.at[pl.ds(s, 1), :], sem.at[0])

    def gather_x(s):
        return pltpu.make_async_copy(x_hbm.at[pl.ds(idx_ref[base + s], 1), :], xrow.at[pl.ds(s, 1), :], sem.at[1])

    def scatter_x(s):
        return pltpu.make_async_copy(xrow.at[pl.ds(s, 1), :], x_hbm.at[pl.ds(idx_ref[base + s], 1), :], sem.at[2])

    def each_row(fn):
        def body(s, c):
            fn(s)
            return c
        lax.fori_loop(0, rows, body, 0, unroll=MOE_DMA_UNROLL)

    @pl.when(f == 0)
    def _():
        each_row(lambda s: (gather_h(s).start(), gather_x(s).start()))
        each_row(lambda s: gather_h(s).wait())
        xb[...] = ys[...].astype(BF16)
        ys[...] = jnp.zeros_like(ys)

    x = xb[...]
    hid = (_silu(jnp.dot(x, wg_ref[0].astype(BF16), preferred_element_type=F32))
           * jnp.dot(x, wu_ref[0].astype(BF16), preferred_element_type=F32))
    ys[...] += jnp.dot(hid.astype(BF16), wd_ref[0].astype(BF16), preferred_element_type=F32)

    @pl.when(f == n_ff - 1)
    def _():
        each_row(lambda s: gather_x(s).wait())
        is_ctx = lax.broadcasted_iota(jnp.int32, (rows, 1), 0) >= cap_lat
        g_mlp = _mod_rows(mod_ref, b, n_batch, 5, d, is_ctx)
        gate = gate_ref[0]
        for j in range(d // 128):
            sl = slice(j * 128, (j + 1) * 128)
            xrow[:, sl] = xrow[:, sl] + g_mlp[:, sl] * (ys[:, sl] * gate)
        each_row(lambda s: scatter_x(s).start())
        each_row(lambda s: scatter_x(s).wait())


def _moe(idx, gate_b, mod, wg, wu, wd, h2, x, *, n_batch, rows, cap_lat):
    m, d = x.shape
    ne, _, ff = wg.shape
    fc = MOE_FF_CHUNK
    n_ff = ff // fc
    kern = functools.partial(_moe_kernel, rows=rows, n_batch=n_batch, cap_lat=cap_lat, d=d, n_ff=n_ff)
    grid_spec = pltpu.PrefetchScalarGridSpec(
        num_scalar_prefetch=1, grid=(ne, n_batch, n_ff),
        in_specs=[pl.BlockSpec((1, rows, 128), lambda e, b, f, idx: (e * n_batch + b, 0, 0)),
                  pl.BlockSpec(mod.shape, lambda e, b, f, idx: (0, 0)),
                  pl.BlockSpec((1, d, fc), lambda e, b, f, idx: (e, 0, f)),
                  pl.BlockSpec((1, d, fc), lambda e, b, f, idx: (e, 0, f)),
                  pl.BlockSpec((1, fc, d), lambda e, b, f, idx: (e, f, 0)),
                  pl.BlockSpec(memory_space=pl.ANY),
                  pl.BlockSpec(memory_space=pl.ANY)],
        out_specs=pl.BlockSpec(memory_space=pl.ANY),
        scratch_shapes=[pltpu.VMEM((rows, d), F32), pltpu.VMEM((rows, d), BF16), pltpu.VMEM((rows, d), F32),
                        pltpu.SemaphoreType.DMA((3,))])
    return pl.pallas_call(
        kern, grid_spec=grid_spec, out_shape=jax.ShapeDtypeStruct((m, d), F32),
        input_output_aliases={7: 0},
        compiler_params=pltpu.CompilerParams(dimension_semantics=("arbitrary",) * 3, vmem_limit_bytes=VMEM_LIMIT,
                                             has_side_effects=True),
        name="moe_experts",
    )(idx, gate_b, mod, wg, wu, wd, h2, x)


def _final_kernel(x_ref, g_ref, o_ref):
    o_ref[0] = _rms(x_ref[0]) * g_ref[...]


def _final_norm(x3, g, *, c_len, seq):
    n_batch, _, d = x3.shape
    tr = _pick(math.gcd(c_len, seq), (256, 128, 64, 32, 16, 8))
    off = c_len // tr
    return pl.pallas_call(
        _final_kernel, grid=(n_batch, seq // tr),
        in_specs=[pl.BlockSpec((1, tr, d), lambda b, i: (b, i + off, 0)), pl.BlockSpec((1, d), lambda b, i: (0, 0))],
        out_specs=pl.BlockSpec((1, tr, d), lambda b, i: (b, i, 0)),
        out_shape=jax.ShapeDtypeStruct((n_batch, seq, d), F32),
        compiler_params=_params(2), name="final_norm",
    )(x3, g.reshape(1, d))


def _block_ones(n, blk):
    i = jnp.arange(n)
    return (i[:, None] // blk == i[None, :] // blk).astype(BF16)


def _rope_tables(seq, c_len):
    rows = seq // GRID_W
    row = jnp.repeat(jnp.arange(rows, dtype=F32), GRID_W)
    col = jnp.tile(jnp.arange(GRID_W, dtype=F32), rows)
    quarter = AT_HEAD_DIM // 4
    freqs = ROPE_THETA ** (-jnp.arange(quarter, dtype=F32) / quarter)
    ang_r, ang_c = row[:, None] * freqs, col[:, None] * freqs
    cos = jnp.concatenate([jnp.cos(ang_r)] * 2 + [jnp.cos(ang_c)] * 2, axis=1)
    sin = jnp.concatenate([-jnp.sin(ang_r), jnp.sin(ang_r), -jnp.sin(ang_c), jnp.sin(ang_c)], axis=1)
    cos = jnp.concatenate([jnp.ones((c_len, AT_HEAD_DIM), F32), cos], axis=0)
    sin = jnp.concatenate([jnp.zeros((c_len, AT_HEAD_DIM), F32), sin], axis=0)
    return cos, sin


def kernel(x, c, ctx, c_ctx, mod_w, mod_b, norm1_g, norm2_g, w_in, rwkv_conv, rwkv_w0, rwkv_w2, rwkv_a0, rwkv_a2, rwkv_g2, rwkv_kk, rwkv_ka, rwkv_rk, rwkv_ln_g, rwkv_ln_b, s5_lam_re, s5_lam_im, s5_log_dt, s5_b_re, s5_b_im, s5_c_re, s5_c_im, s5_d, s5_glu, attn_qn, attn_kn, w_branch, w_out, router, exp_gate, exp_up, exp_down, final_g):
    n_batch, seq, d = x.shape
    c_len = ctx.shape[1]
    depth = mod_w.shape[0]
    tb = c_len + seq
    m = n_batch * tb
    tc = math.gcd(c_len, 256)
    ne = router.shape[2]
    cap_ctx, cap_lat = CAPACITY_FACTOR * c_len // ne, CAPACITY_FACTOR * seq // ne
    rows = cap_ctx + cap_lat

    xa = jnp.concatenate([ctx, x], axis=1).reshape(m, d)
    c_all_t = jnp.concatenate([c, c_ctx[None]], axis=0).T
    mods = _modulation(c_all_t, mod_w, mod_b)
    ones_head = _block_ones(RW_WIDTH, RW_HEAD_DIM)
    tri = (jnp.arange(128)[:, None] <= jnp.arange(128)[None, :]).astype(BF16)
    cos_t, sin_t = _rope_tables(seq, c_len)
    row_base = (jnp.arange(n_batch, dtype=jnp.int32) * tb)[None, :, None]
    seg_base = jnp.concatenate([jnp.full((cap_lat,), c_len, jnp.int32), jnp.zeros((cap_ctx,), jnp.int32)])[None, None, :]

    for l in range(depth):
        mod = mods[l]
        p = _inproj(xa, mod, norm1_g[l], w_in[l], tb=tb, c_len=c_len, n_batch=n_batch)

        r, v, kk, w0, kd0, b0, w1, kd1, b1, bonus, g = _rwkv_prep(
            p, rwkv_conv[l], rwkv_w0[l], rwkv_w2[l], rwkv_a0[l], rwkv_a2[l], rwkv_g2[l], rwkv_kk[l],
            rwkv_ka[l], rwkv_rk[l].reshape(-1), ones_head, tb=tb, c_len=c_len)
        o0, o1 = _rwkv_scan(r, v, kk, w0, kd0, b0, w1, kd1, b1, n_batch=n_batch, tb=tb, c_len=c_len)
        y_rw = _rwkv_finish(o0, o1, bonus, g, rwkv_ln_g[l], rwkv_ln_b[l], ones_head, tb=tb)

        a_rows, bres, bims, cres, cims = [], [], [], [], []
        for z in range(2):
            ab_re, ab_im, bb_re, bb_im = _s5_discretize(s5_lam_re[l, z], s5_lam_im[l, z], s5_log_dt[l, z],
                                                        s5_b_re[l, z], s5_b_im[l, z])
            a_rows += [ab_re.reshape(1, S5_NX), ab_im.reshape(1, S5_NX)]
            bres.append(_s5_pack_in(bb_re))
            bims.append(_s5_pack_in(bb_im))
            cres.append(_s5_pack_out(s5_c_re[l, z]))
            cims.append(_s5_pack_out(s5_c_im[l, z]))
        stack = lambda ts: jnp.stack(ts).astype(BF16)
        ys0, ys1 = _s5_scan(p, jnp.concatenate(a_rows, axis=0), stack(bres), stack(bims), stack(cres), stack(cims),
                            n_batch=n_batch, tb=tb, c_len=c_len, tc=tc)
        y_s5 = _s5_finish(ys0, ys1, p, s5_d[l], s5_glu[l].astype(BF16), tb=tb)

        q_r, k_r, v_b = _attn_prep(p, cos_t, sin_t, attn_qn[l], attn_kn[l], tb=tb)
        y_at = _flash(q_r, k_r, v_b, n_batch=n_batch, tb=tb, c_len=c_len)

        mm = _merge(y_rw, y_s5, y_at, p, w_branch[l].astype(BF16), tb=tb)
        x1 = _outproj(mm, w_out[l].astype(BF16), xa, mod, tb=tb, c_len=c_len, n_batch=n_batch)

        h2, aff_t = _router(x1, mod, norm2_g[l], router[l].T, tb=tb, c_len=c_len, n_batch=n_batch)
        pos = _select(aff_t, tri, n_batch=n_batch, tb=tb, c_len=c_len, caps=(cap_ctx, cap_lat))[..., 0]
        idx = pos + seg_base + row_base
        gate = jnp.take_along_axis(aff_t, idx.reshape(ne, -1), axis=1)
        gate_b = jnp.broadcast_to(gate.reshape(ne * n_batch, rows, 1), (ne * n_batch, rows, 128))
        xa = _moe(idx.reshape(-1), gate_b, mod, exp_gate[l], exp_up[l], exp_down[l], h2, x1,
                  n_batch=n_batch, rows=rows, cap_lat=cap_lat)

    return _final_norm(xa.reshape(n_batch, tb, d), final_g, c_len=c_len, seq=seq)
```

```python
import functools
import math

import jax
import jax.numpy as jnp
from jax import lax
from jax.experimental import pallas as pl
from jax.experimental.pallas import tpu as pltpu

F32 = jnp.float32
BF16 = jnp.bfloat16

NORM_EPS = 1e-6
N_MOD = 6
GRID_W = 64

RW_HEADS = 8
RW_HEAD_DIM = 64
RW_WIDTH = 512
RW_RANK = 64
RW_GATE_RANK = 128
RW_LN_EPS = 64e-5

S5_WIDTH = 512
S5_GROUP = 16
S5_GROUPS = 32
S5_STATE = 64
S5_NX = S5_GROUPS * S5_STATE
S5_PACKS = 4

AT_HEADS = 8
AT_KV_HEADS = 2
AT_GROUP = AT_HEADS // AT_KV_HEADS
AT_HEAD_DIM = 128
AT_WIDTH = 1024
AT_KV_WIDTH = 256
ROPE_THETA = 10000.0

N_EXPERTS = 16
CAPACITY_FACTOR = 2

OFF_WD = 3 * RW_WIDTH
OFF_S5 = OFF_WD + 4 * RW_RANK + RW_GATE_RANK
OFF_Q = OFF_S5 + S5_WIDTH
OFF_K = OFF_Q + AT_WIDTH
OFF_GATE = OFF_K + 2 * AT_KV_WIDTH

P_RW = 0
P_S5 = 2048
P_KV = 2560
P_Q = 3072
P_GATE = 4096

VMEM_LIMIT = 56 * 1024 * 1024


def _params(n_axes, vmem=VMEM_LIMIT):
    return pltpu.CompilerParams(dimension_semantics=("arbitrary",) * n_axes, vmem_limit_bytes=vmem)


def _pick(n, cands):
    for t in cands:
        if n % t == 0:
            return t
    raise ValueError(f"no tile for {n}")


def _split3(x):
    hi = x.astype(BF16)
    r1 = x - hi.astype(F32)
    mid = r1.astype(BF16)
    lo = (r1 - mid.astype(F32)).astype(BF16)
    return hi, mid, lo


def _dot_exact_rhs(x, w_bf16, pieces=3):
    parts = _split3(x)[:pieces]
    acc = jnp.dot(parts[0], w_bf16, preferred_element_type=F32)
    for p in parts[1:]:
        acc = acc + jnp.dot(p, w_bf16, preferred_element_type=F32)
    return acc


def _dot_f32(a, b, dims=(((1,), (0,)), ((), ()))):
    a1, a2, a3 = _split3(a)
    b1, b2, b3 = _split3(b)
    d = functools.partial(lax.dot_general, dimension_numbers=dims, preferred_element_type=F32)
    return (d(a1, b1) + (d(a1, b2) + d(a2, b1))) + ((d(a1, b3) + d(a3, b1)) + d(a2, b2))


def _sigmoid(x):
    return 1.0 / (1.0 + jnp.exp(-x))


def _silu(x):
    return x * _sigmoid(x)


def _row_is_ctx(i, tiles_per_batch, tm, c_len):
    pos = (i % tiles_per_batch) * tm + lax.broadcasted_iota(jnp.int32, (tm, 1), 0)
    return pos < c_len


def _mod_rows(mod_ref, b, n_batch, which, d, is_ctx):
    lat = mod_ref[pl.ds(b, 1), which * d:(which + 1) * d]
    ctx = mod_ref[n_batch:n_batch + 1, which * d:(which + 1) * d]
    return jnp.where(is_ctx, ctx, lat)


def _rms(x):
    return x * lax.rsqrt(jnp.mean(x * x, axis=-1, keepdims=True) + NORM_EPS)


def _mod_kernel(ct_ref, w_ref, b_ref, o_ref, *, n_rows):
    w = w_ref[0]
    for m in range(n_rows):
        col = ct_ref[:, m:m + 1]
        o_ref[0, m:m + 1, :] = jnp.sum(w * _silu(col), axis=0, keepdims=True) + b_ref[0]


def _modulation(c_all_t, mod_w, mod_b):
    depth, d, n6 = mod_w.shape
    n_rows = c_all_t.shape[1]
    tn = 512
    return pl.pallas_call(
        functools.partial(_mod_kernel, n_rows=n_rows),
        grid=(depth, n6 // tn),
        in_specs=[pl.BlockSpec((d, n_rows), lambda l, n: (0, 0)),
                  pl.BlockSpec((1, d, tn), lambda l, n: (l, 0, n)),
                  pl.BlockSpec((1, 1, tn), lambda l, n: (l, 0, n))],
        out_specs=pl.BlockSpec((1, n_rows, tn), lambda l, n: (l, 0, n)),
        out_shape=jax.ShapeDtypeStruct((depth, n_rows, n6), F32),
        compiler_params=_params(2), name="modulation",
    )(c_all_t, mod_w, mod_b.reshape(depth, 1, n6))


def _inproj_kernel(x_ref, mod_ref, g_ref, w_ref, o_ref, h_ref, *, tpb, tm, c_len, n_batch, d):
    i = pl.program_id(0)

    @pl.when(pl.program_id(1) == 0)
    def _():
        b = i // tpb
        n_chunks = next(n for n in (4, 3, 2, 1) if tm % (16 * n) == 0)
        rc = tm // n_chunks
        for j in range(n_chunks):
            rows = slice(j * rc, (j + 1) * rc)
            is_ctx = _row_is_ctx(i, tpb, tm, c_len)[rows]
            shift = _mod_rows(mod_ref, b, n_batch, 0, d, is_ctx)
            scale = _mod_rows(mod_ref, b, n_batch, 1, d, is_ctx)
            h = _rms(x_ref[rows, :]) * g_ref[...]
            h_ref[rows, :] = (h * (1 + scale) + shift).astype(BF16)

    o_ref[...] = jnp.dot(h_ref[...], w_ref[0].astype(BF16), preferred_element_type=F32)


INPROJ_TN = 512


def _w_in_offset(n):
    u = 128
    t = INPROJ_TN // u
    units = jnp.where(n < P_S5 // INPROJ_TN, n * t,
            jnp.where(n < P_KV // INPROJ_TN, OFF_S5 // u + (n - P_S5 // INPROJ_TN) * t,
            jnp.where(n < P_Q // INPROJ_TN, OFF_K // u + (n - P_KV // INPROJ_TN) * t,
            jnp.where(n < P_GATE // INPROJ_TN, OFF_Q // u + (n - P_Q // INPROJ_TN) * t,
                      OFF_GATE // u + (n - P_GATE // INPROJ_TN) * t))))
    return units * u


def _inproj(x, mod, g, w_in, layer, *, tb, c_len, n_batch):
    m, d = x.shape
    n_out = P_GATE + w_in.shape[2] - OFF_GATE
    tm = _pick(tb, (1056, 1024, 768, 640, 512, 256, 128))
    tn = INPROJ_TN
    kern = functools.partial(_inproj_kernel, tpb=tb // tm, tm=tm, c_len=c_len, n_batch=n_batch, d=d)
    return pl.pallas_call(
        kern, grid=(m // tm, n_out // tn),
        in_specs=[pl.BlockSpec((tm, d), lambda i, n: (i, 0)),
                  pl.BlockSpec(mod.shape, lambda i, n: (0, 0)),
                  pl.BlockSpec((1, d), lambda i, n: (0, 0)),
                  pl.BlockSpec((pl.Element(1), pl.Element(d), pl.Element(tn)), lambda i, n: (layer, 0, _w_in_offset(n)))],
        out_specs=pl.BlockSpec((tm, tn), lambda i, n: (i, n)),
        out_shape=jax.ShapeDtypeStruct((m, n_out), F32),
        scratch_shapes=[pltpu.VMEM((tm, d), BF16)],
        compiler_params=_params(2), name="inproj",
    )(x, mod, g.reshape(1, d), w_in)


def _softplus(z):
    return jnp.maximum(z, 0.0) + jnp.log(1.0 + jnp.exp(-jnp.abs(z)))


def _rwkv_prep_kernel(p_ref, prev_ref, next_ref, conv_ref, w0_ref, w2_ref, a0_ref, a2_ref, g2_ref,
                      kk_ref, ka_ref, rk_ref, ones_ref,
                      r_o, v_o, kk_o, w0_o, kd0_o, b0_o, w1_o, kd1_o, b1_o, bonus_o, g_o,
                      *, tr, tpb, c_len, tb):
    i = pl.program_id(0)
    c = RW_WIDTH
    pos = (i % tpb) * tr + lax.broadcasted_iota(jnp.int32, (tr, 1), 0)
    row = lax.broadcasted_iota(jnp.int32, (tr, 1), 0)
    first = (pos == 0) | (pos == c_len)
    last = (pos == c_len - 1) | (pos == tb - 1)

    x = p_ref[:, 0:3 * c]
    x_prev = jnp.where(row == 0, prev_ref[7:8, 0:3 * c], pltpu.roll(x, 1, 0))
    x_next = jnp.where(row == tr - 1, next_ref[0:1, 0:3 * c], pltpu.roll(x, tr - 1, 0))
    x_prev = jnp.where(first, 0.0, x_prev)
    x_next = jnp.where(last, 0.0, x_next)
    rkv = x_prev * conv_ref[0:1, :] + x * conv_ref[1:2, :] + x_next * conv_ref[2:3, :]
    r, k, v = rkv[:, 0:c], rkv[:, c:2 * c], rkv[:, 2 * c:3 * c]
    ones = ones_ref[...]

    kk = k * kk_ref[...]
    kk = kk * lax.rsqrt(_dot_exact_rhs(kk * kk, ones) + 1e-12)
    r_o[...] = r
    v_o[...] = v
    kk_o[...] = kk

    kd_sum = jnp.zeros_like(k)
    for z, (w_o, kd_o, b_o) in enumerate(((w0_o, kd0_o, b0_o), (w1_o, kd1_o, b1_o))):
        wd = jnp.tanh(p_ref[:, OFF_WD + z * RW_RANK:OFF_WD + (z + 1) * RW_RANK])
        ad = p_ref[:, OFF_WD + (2 + z) * RW_RANK:OFF_WD + (3 + z) * RW_RANK]
        w_lin = w0_ref[z:z + 1, :] + _dot_f32(wd, w2_ref[z])
        w_log = -_softplus(-w_lin) - 0.5
        w_o[...] = -jnp.exp(w_log)
        a = _sigmoid(a0_ref[z:z + 1, :] + _dot_f32(ad, a2_ref[z]))
        kd = k * (1 + (a - 1) * ka_ref[...])
        kd_o[...] = kd
        b_o[...] = kk * a
        kd_sum = kd_sum + kd

    bonus_o[...] = _dot_exact_rhs(r * kd_sum * rk_ref[...], ones) * v
    gd = _sigmoid(p_ref[:, OFF_WD + 4 * RW_RANK:OFF_WD + 4 * RW_RANK + RW_GATE_RANK])
    g_o[...] = _dot_f32(gd, g2_ref[...])


def _rwkv_prep(p, conv_w, w0, w2, a0, a2, g2, k_k, k_a, r_k, ones_bd, *, tb, c_len):
    m = p.shape[0]
    c = RW_WIDTH
    tr = _pick(tb, (528, 512, 320, 256, 128))
    nblk8 = m // 8
    kern = functools.partial(_rwkv_prep_kernel, tr=tr, tpb=tb // tr, c_len=c_len, tb=tb)
    full = lambda a: pl.BlockSpec(a.shape, lambda i: (0,) * a.ndim)
    args = (conv_w, w0, w2, a0, a2, g2, k_k.reshape(1, c), k_a.reshape(1, c), r_k.reshape(1, c), ones_bd)
    out = jax.ShapeDtypeStruct((m, c), F32)
    return pl.pallas_call(
        kern, grid=(m // tr,),
        in_specs=[pl.BlockSpec((tr, 2048), lambda i: (i, 0)),
                  pl.BlockSpec((8, 2048), lambda i: (jnp.maximum(i * (tr // 8) - 1, 0), 0)),
                  pl.BlockSpec((8, 2048), lambda i: (jnp.minimum((i + 1) * (tr // 8), nblk8 - 1), 0))]
                 + [full(a) for a in args],
        out_specs=[pl.BlockSpec((tr, c), lambda i: (i, 0))] * 11,
        out_shape=[out] * 11,
        compiler_params=_params(1), name="rwkv_prep",
    )(p, p, p, *args)


def _rev_chunk(c, n_ctx_chunks, n_chunks):
    return jnp.where(c < n_ctx_chunks, n_ctx_chunks - 1 - c, n_chunks - 1 - (c - n_ctx_chunks))


RW_CHUNK = 64

_NN = (((1,), (0,)), ((), ()))
_NT = (((1,), (1,)), ((), ()))
_TN = (((0,), (0,)), ((), ()))


def _mm(a, b, dims=_NN, pa=1, pb=1):
    xs, ys = _split3(a)[:pa], _split3(b)[:pb]
    acc = None
    for i, x in enumerate(xs):
        for j, y in enumerate(ys):
            if i + j < max(pa, pb):
                d = lax.dot_general(x, y, dims, preferred_element_type=F32)
                acc = d if acc is None else acc + d
    return acc


_RW_PIECES = {"gram": (1, 1), "inverse": (2, 2), "values": (1, 1), "solve": (1, 1), "readout": (1, 1),
              "carry": (1, 1), "state": (2, 2)}


def _rwkv_scan_kernel(r0, v0, k0, w0, d0, b0, r1, v1, k1, w1, d1, b1, o0, o1, h_ref):
    t = RW_CHUNK
    t2 = 2 * t

    @pl.when(pl.program_id(1) == 0)
    def _():
        h_ref[...] = jnp.zeros_like(h_ref)

    r2 = lax.broadcasted_iota(jnp.int32, (t2, t2), 0)
    c2 = lax.broadcasted_iota(jnp.int32, (t2, t2), 1)
    same = (r2 // t) == (c2 // t)
    eye = r2 == c2
    row = lax.broadcasted_iota(jnp.int32, (t, t), 0)
    col = lax.broadcasted_iota(jnp.int32, (t, t), 1)
    head0 = lax.broadcasted_iota(jnp.int32, (t, t2), 1) < RW_HEAD_DIM

    def stack(x):
        return jnp.concatenate([jnp.where(head0, x, 0.0), jnp.where(head0, 0.0, x)], axis=0)

    combos = []
    for z, (r, v, kk, w, kd, bb, o) in enumerate(((r0, v0, k0, w0, d0, b0, o0), (r1, v1, k1, w1, d1, b1, o1))):
        before = (c2 % t < r2 % t) if z == 0 else (c2 % t > r2 % t)
        strict = same & before
        incl = same & (before | eye)
        cum = jnp.where((col <= row) if z == 0 else (col >= row), 1.0, 0.0)
        lw_all = w[...]
        g_all = _mm(cum, lw_all, pb=3)
        gtot_all = g_all[t - 1:t, :] if z == 0 else g_all[0:1, :]
        for hp in range(RW_WIDTH // t2):
            sl = slice(hp * t2, (hp + 1) * t2)
            g = g_all[:, sl]
            e_inv = jnp.exp(-g)
            combos.append(dict(
                z=z, hp=hp, sl=sl, o=o, strict=strict, incl=incl, gtot=jnp.exp(gtot_all[:, sl]),
                a_st=stack(kk[:, sl] * jnp.exp(g - lw_all[:, sl])), r_st=stack(r[:, sl] * jnp.exp(g)),
                bt=bb[:, sl] * e_inv, kt=kd[:, sl] * e_inv, v_st=stack(v[:, sl])))

    for c in combos:
        mm = _mm(jnp.concatenate([c["a_st"], c["r_st"]], axis=0),
                 jnp.concatenate([c["bt"], c["bt"], c["kt"], c["kt"]], axis=0), _NT, *_RW_PIECES["gram"])
        c["n_pow"] = jnp.where(c["strict"], -mm[:t2, :t2], 0.0)
        c["l_ak"] = jnp.where(c["strict"], mm[:t2, t2:], 0.0)
        c["m_rb"] = jnp.where(c["incl"], mm[t2:, :t2], 0.0)
        c["m_rk"] = jnp.where(c["incl"], mm[t2:, t2:], 0.0)
        c["inv"] = jnp.where(eye, 1.0, 0.0) + c["n_pow"]
    for c in combos:
        c["y"] = _mm(jnp.concatenate([c["l_ak"], c["m_rk"]], axis=0), c["v_st"], _NN, *_RW_PIECES["values"])
    for _ in range(5):
        for c in combos:
            c["n_pow"] = _mm(c["n_pow"], c["n_pow"], _NN, *_RW_PIECES["inverse"])
        for c in combos:
            c["inv"] = c["inv"] + _mm(c["inv"], c["n_pow"], _NN, *_RW_PIECES["inverse"])
    for c in combos:
        c["zz"] = _mm(c["inv"], jnp.concatenate([c["a_st"], c["y"][:t2]], axis=1), _NN, *_RW_PIECES["solve"])
    for c in combos:
        ro = _mm(c["m_rb"], c["zz"], _NN, *_RW_PIECES["readout"])
        c["r_hat"] = c["r_st"] - ro[:, :t2]
        c["o_hat"] = c["y"][t2:] - ro[:, t2:]
    for c in combos:
        btw = _mm(stack(c["bt"] * c["gtot"]), c["zz"], _TN, *_RW_PIECES["carry"])
        c["p"] = jnp.where(eye, c["gtot"], 0.0) - btw[:, :t2]
        c["q"] = _mm(stack(c["kt"] * c["gtot"]), c["v_st"], _TN, *_RW_PIECES["carry"]) - btw[:, t2:]
    for c in combos:
        h = h_ref[c["z"], c["hp"]]
        o_st = _mm(c["r_hat"], h, _NN, *_RW_PIECES["state"]) + c["o_hat"]
        c["o"][:, c["sl"]] = o_st[:t] + o_st[t:]
        h_ref[c["z"], c["hp"]] = _mm(c["p"], h, _NN, *_RW_PIECES["state"]) + c["q"]


def _rwkv_scan(r, v, kk, w0, kd0, b0, w1, kd1, b1, *, n_batch, tb, c_len):
    m, c = r.shape
    tc = RW_CHUNK
    nc, ncc = tb // tc, c_len // tc
    fwd = pl.BlockSpec((tc, c), lambda b, j: (b * nc + j, 0))
    rev = pl.BlockSpec((tc, c), lambda b, j: (b * nc + _rev_chunk(j, ncc, nc), 0))
    out = jax.ShapeDtypeStruct((m, c), F32)
    return pl.pallas_call(
        _rwkv_scan_kernel, grid=(n_batch, nc),
        in_specs=[fwd] * 6 + [rev] * 6,
        out_specs=[fwd, rev], out_shape=[out, out],
        scratch_shapes=[pltpu.VMEM((2, c // (2 * RW_HEAD_DIM), 2 * RW_HEAD_DIM, 2 * RW_HEAD_DIM), F32)],
        compiler_params=_params(2), name="rwkv_scan",
    )(r, v, kk, w0, kd0, b0, r, v, kk, w1, kd1, b1)


def _rwkv_finish_kernel(o0, o1, bonus, g, lng, lnb, ones_ref, y_ref):
    o = o0[...] + o1[...]
    ones = ones_ref[...]
    mu = _dot_exact_rhs(o, ones) * (1.0 / RW_HEAD_DIM)
    dev = o - mu
    var = _dot_exact_rhs(dev * dev, ones) * (1.0 / RW_HEAD_DIM)
    o_n = dev * lax.rsqrt(var + RW_LN_EPS)
    y_ref[...] = ((o_n * lng[...] + lnb[...] + bonus[...]) * g[...]).astype(BF16)


def _rwkv_finish(o0, o1, bonus, g, ln_g, ln_b, ones_bd, *, tb):
    m, c = o0.shape
    tr = _pick(tb, (1056, 1024, 640, 512, 256, 128))
    blk = pl.BlockSpec((tr, c), lambda i: (i, 0))
    vec = pl.BlockSpec((1, c), lambda i: (0, 0))
    return pl.pallas_call(
        _rwkv_finish_kernel, grid=(m // tr,),
        in_specs=[blk] * 4 + [vec, vec, pl.BlockSpec(ones_bd.shape, lambda i: (0, 0))],
        out_specs=blk, out_shape=jax.ShapeDtypeStruct((m, c), BF16),
        compiler_params=_params(1), name="rwkv_finish",
    )(o0, o1, bonus, g, ln_g.reshape(1, c), ln_b.reshape(1, c), ones_bd)


def _s5_discretize(lam_re, lam_im, log_dt, b_re, b_im):
    dt = jnp.exp(log_dt)[:, None]
    mag = jnp.exp(lam_re * dt)
    ab_re, ab_im = mag * jnp.cos(lam_im * dt), mag * jnp.sin(lam_im * dt)
    den = lam_re * lam_re + lam_im * lam_im
    num_re = ab_re - 1.0
    co_re = (num_re * lam_re + ab_im * lam_im) / den
    co_im = (ab_im * lam_re - num_re * lam_im) / den
    bb_re = co_re[..., None] * b_re - co_im[..., None] * b_im
    bb_im = co_re[..., None] * b_im + co_im[..., None] * b_re
    return ab_re, ab_im, bb_re, bb_im


def _s5_powers(lam_re, lam_im, log_dt, n):
    dt = jnp.exp(log_dt)[None, :, None]
    k = jnp.arange(1, n + 1, dtype=F32)[:, None, None]
    mag = jnp.exp(k * lam_re[None] * dt)
    ang = k * lam_im[None] * dt
    return [(mag * jnp.cos(ang)).reshape(n, S5_NX), (mag * jnp.sin(ang)).reshape(n, S5_NX)]


def _s5_pack_in(bb):
    gpp = S5_GROUPS // S5_PACKS
    bb = bb.reshape(S5_PACKS, gpp, S5_STATE, S5_GROUP)
    eye = jnp.eye(gpp, dtype=bb.dtype)
    t = jnp.einsum('qgnc,gh->qgchn', bb, eye)
    return t.reshape(S5_PACKS, gpp * S5_GROUP, gpp * S5_STATE)


def _s5_pack_out(cc):
    gpp = S5_GROUPS // S5_PACKS
    cc = cc.reshape(S5_PACKS, gpp, S5_GROUP, S5_STATE)
    eye = jnp.eye(gpp, dtype=cc.dtype)
    t = jnp.einsum('qgcn,gh->qgnhc', cc, eye)
    return t.reshape(S5_PACKS, gpp * S5_STATE, gpp * S5_GROUP)


S5_SEGMENTS = 8


def _cmul(ar, ai, br, bi):
    return ar * br - ai * bi, ar * bi + ai * br


def _s5_scan_kernel(u0, u1, a_ref, pw_ref, perm_ref, bre, bim, cre, cim, y0, y1, xr_ref, xi_ref, st_ref, *, tc):
    @pl.when(pl.program_id(1) == 0)
    def _():
        st_ref[...] = jnp.zeros_like(st_ref)

    cw = S5_WIDTH // S5_PACKS
    nw = S5_NX // S5_PACKS
    n_seg = S5_SEGMENTS
    seg = tc // n_seg
    sub = lax.broadcasted_iota(jnp.int32, (n_seg, S5_NX), 0)
    for z, (u, y) in enumerate(((u0, y0), (u1, y1))):
        perm = perm_ref[z]
        ub = jnp.dot(perm, u[...].astype(BF16), preferred_element_type=F32).astype(BF16)
        for q in range(S5_PACKS):
            uq = ub[:, q * cw:(q + 1) * cw]
            xr_ref[:, q * nw:(q + 1) * nw] = jnp.dot(uq, bre[z, q], preferred_element_type=F32)
            xi_ref[:, q * nw:(q + 1) * nw] = jnp.dot(uq, bim[z, q], preferred_element_type=F32)
        ar = a_ref[2 * z:2 * z + 1, :]
        ai = a_ref[2 * z + 1:2 * z + 2, :]

        sr = jnp.zeros((n_seg, S5_NX), F32)
        si = jnp.zeros((n_seg, S5_NX), F32)
        for i in range(seg):
            rows = slice(i * n_seg, (i + 1) * n_seg)
            pr, pi = _cmul(ar, ai, sr, si)
            sr, si = pr + xr_ref[rows, :], pi + xi_ref[rows, :]
            xr_ref[rows, :] = sr
            xi_ref[rows, :] = si

        a_seg_r = pw_ref[2 * z, seg - 1:seg, :]
        a_seg_i = pw_ref[2 * z + 1, seg - 1:seg, :]
        cr, ci = st_ref[2 * z:2 * z + 1, :], st_ref[2 * z + 1:2 * z + 2, :]
        cin_r = jnp.zeros((n_seg, S5_NX), F32)
        cin_i = jnp.zeros((n_seg, S5_NX), F32)
        for k in range(n_seg):
            sg = k if z == 0 else n_seg - 1 - k
            cin_r = jnp.where(sub == sg, cr, cin_r)
            cin_i = jnp.where(sub == sg, ci, cin_i)
            pr, pi = _cmul(a_seg_r, a_seg_i, cr, ci)
            cr, ci = pr + sr[sg:sg + 1, :], pi + si[sg:sg + 1, :]
        st_ref[2 * z:2 * z + 1, :] = cr
        st_ref[2 * z + 1:2 * z + 2, :] = ci

        for i in range(seg):
            rows = slice(i * n_seg, (i + 1) * n_seg)
            pr, pi = _cmul(pw_ref[2 * z, i:i + 1, :], pw_ref[2 * z + 1, i:i + 1, :], cin_r, cin_i)
            xr_ref[rows, :] = xr_ref[rows, :] + pr
            xi_ref[rows, :] = xi_ref[rows, :] + pi

        xr = xr_ref[...].astype(BF16)
        xi = xi_ref[...].astype(BF16)
        yp = jnp.concatenate(
            [jnp.dot(xr[:, q * nw:(q + 1) * nw], cre[z, q], preferred_element_type=F32)
             - jnp.dot(xi[:, q * nw:(q + 1) * nw], cim[z, q], preferred_element_type=F32) for q in range(S5_PACKS)], axis=1)
        y[...] = _mm(perm.astype(F32), yp, _TN, 1, 3)


def _s5_perms(tc):
    n_seg = S5_SEGMENTS
    seg = tc // n_seg
    i = jnp.arange(tc) // n_seg
    sg = jnp.arange(tc) % n_seg
    cols = jnp.stack([sg * seg + i, sg * seg + (seg - 1 - i)])
    return (cols[:, :, None] == jnp.arange(tc)[None, None, :]).astype(BF16)


def _s5_scan(p, a_all, pw, perms, bre, bim, cre, cim, *, n_batch, tb, c_len, tc):
    m = p.shape[0]
    nc, ncc = tb // tc, c_len // tc
    ucol = P_S5 // S5_WIDTH
    full = lambda a: pl.BlockSpec(a.shape, lambda b, j: (0,) * a.ndim)
    out = jax.ShapeDtypeStruct((m, S5_WIDTH), F32)
    return pl.pallas_call(
        functools.partial(_s5_scan_kernel, tc=tc), grid=(n_batch, nc),
        in_specs=[pl.BlockSpec((tc, S5_WIDTH), lambda b, j: (b * nc + j, ucol)),
                  pl.BlockSpec((tc, S5_WIDTH), lambda b, j: (b * nc + _rev_chunk(j, ncc, nc), ucol)),
                  full(a_all), full(pw), full(perms), full(bre), full(bim), full(cre), full(cim)],
        out_specs=[pl.BlockSpec((tc, S5_WIDTH), lambda b, j: (b * nc + j, 0)),
                   pl.BlockSpec((tc, S5_WIDTH), lambda b, j: (b * nc + _rev_chunk(j, ncc, nc), 0))],
        out_shape=[out, out],
        scratch_shapes=[pltpu.VMEM((tc, S5_NX), F32), pltpu.VMEM((tc, S5_NX), F32), pltpu.VMEM((4, S5_NX), F32)],
        compiler_params=_params(2), name="s5_scan",
    )(p, p, a_all, pw, perms, bre, bim, cre, cim)


def _s5_finish_kernel(y0, y1, u, d_ref, w_ref, o_ref):
    y = y0[...] + y1[...] + d_ref[...] * u[...]
    ge = 0.5 * y * (1.0 + jnp.tanh(math.sqrt(2.0 / math.pi) * (y + 0.044715 * (y * y * y))))
    gate = _sigmoid(jnp.dot(ge.astype(BF16), w_ref[...], preferred_element_type=F32))
    o_ref[...] = (ge * gate).astype(BF16)


def _s5_finish(y0, y1, p, d_skip, w_glu, *, tb):
    m, c = y0.shape
    tr = _pick(tb, (1056, 1024, 640, 512, 256, 128))
    blk = pl.BlockSpec((tr, c), lambda i: (i, 0))
    return pl.pallas_call(
        _s5_finish_kernel, grid=(m // tr,),
        in_specs=[blk, blk, pl.BlockSpec((tr, c), lambda i: (i, P_S5 // c)),
                  pl.BlockSpec((1, c), lambda i: (0, 0)), pl.BlockSpec((c, c), lambda i: (0, 0))],
        out_specs=blk, out_shape=jax.ShapeDtypeStruct((m, c), BF16),
        compiler_params=_params(1), name="s5_finish",
    )(y0, y1, p, d_skip.reshape(1, c), w_glu)


def _rope(x, cos, sin_signed):
    lane = lax.broadcasted_iota(jnp.int32, x.shape, 1)
    swapped = jnp.where(lane % 64 < 32, pltpu.roll(x, 96, 1), pltpu.roll(x, 32, 1))
    return x * cos + swapped * sin_signed


def _attn_prep_kernel(kv_ref, q_ref, cos_ref, sin_ref, qn_ref, kn_ref, qo, ko, vo):
    cos, sin = cos_ref[...], sin_ref[...]
    hd = AT_HEAD_DIM
    for h in range(AT_HEADS):
        q = _rms(q_ref[:, h * hd:(h + 1) * hd]) * qn_ref[...]
        qo[:, h * hd:(h + 1) * hd] = (_rope(q, cos, sin) * (hd ** -0.5 * math.log2(math.e))).astype(BF16)
    for h in range(AT_KV_HEADS):
        k = _rms(kv_ref[:, h * hd:(h + 1) * hd]) * kn_ref[...]
        ko[:, h * hd:(h + 1) * hd] = _rope(k, cos, sin).astype(BF16)
    vo[...] = kv_ref[:, AT_KV_WIDTH:2 * AT_KV_WIDTH].astype(BF16)


def _attn_prep(p, cos_t, sin_t, qn, kn, *, tb):
    m = p.shape[0]
    tr = _pick(tb, (528, 512, 320, 256, 128))
    tpb = tb // tr
    hd = AT_HEAD_DIM
    return pl.pallas_call(
        _attn_prep_kernel, grid=(m // tr,),
        in_specs=[pl.BlockSpec((tr, 2 * AT_KV_WIDTH), lambda i: (i, P_KV // (2 * AT_KV_WIDTH))),
                  pl.BlockSpec((tr, AT_WIDTH), lambda i: (i, P_Q // AT_WIDTH)),
                  pl.BlockSpec((tr, hd), lambda i: (i % tpb, 0)),
                  pl.BlockSpec((tr, hd), lambda i: (i % tpb, 0)),
                  pl.BlockSpec((1, hd), lambda i: (0, 0)), pl.BlockSpec((1, hd), lambda i: (0, 0))],
        out_specs=[pl.BlockSpec((tr, AT_WIDTH), lambda i: (i, 0)),
                   pl.BlockSpec((tr, AT_KV_WIDTH), lambda i: (i, 0)),
                   pl.BlockSpec((tr, AT_KV_WIDTH), lambda i: (i, 0))],
        out_shape=[jax.ShapeDtypeStruct((m, AT_WIDTH), BF16), jax.ShapeDtypeStruct((m, AT_KV_WIDTH), BF16),
                   jax.ShapeDtypeStruct((m, AT_KV_WIDTH), BF16)],
        compiler_params=_params(1), name="attn_prep",
    )(p, p, cos_t, sin_t, qn.reshape(1, hd), kn.reshape(1, hd))


FLASH_TQ = (1056, 1024, 640, 512, 256, 128)
FLASH_TK = (768, 640, 512, 384, 256, 128)


def _flash_kernel(q_ref, k_ref, v_ref, o_ref, m_ref, acc_ref, s_ref, p_ref, *, tq, tk, c_len, nk):
    qi, ki = pl.program_id(2), pl.program_id(3)
    hd = AT_HEAD_DIM

    @pl.when(ki == 0)
    def _():
        m_ref[...] = jnp.full_like(m_ref, -1e30)
        acc_ref[...] = jnp.zeros_like(acc_ref)

    def body(masked):
        k = k_ref[...]
        v_ext = jnp.concatenate([v_ref[...], jnp.ones((tk, hd), BF16)], axis=1)
        if masked:
            row = qi * tq + lax.broadcasted_iota(jnp.int32, (tq, 1), 0)
            col = ki * tk + lax.broadcasted_iota(jnp.int32, (1, tk), 1)
            hide = (row < c_len) & (col >= c_len)
        for g in range(AT_GROUP):
            b = g % 2
            s = lax.dot_general(q_ref[:, g * hd:(g + 1) * hd], k, _NT, preferred_element_type=F32)
            s_ref[b] = jnp.where(hide, -1e30, s) if masked else s
            m_prev = m_ref[g]
            m_new = jnp.maximum(m_prev, jnp.max(s_ref[b], axis=-1, keepdims=True))
            m_ref[g] = m_new
            p_ref[b] = jnp.exp2(s_ref[b] - m_new).astype(BF16)
            acc_ref[g] = jnp.exp2(m_prev - m_new) * acc_ref[g] + jnp.dot(p_ref[b], v_ext, preferred_element_type=F32)

    has_ctx = qi * tq < c_len
    pl.when(has_ctx)(lambda: body(True))
    pl.when(jnp.logical_not(has_ctx))(lambda: body(False))

    @pl.when(ki == nk - 1)
    def _():
        for g in range(AT_GROUP):
            o_ref[:, g * hd:(g + 1) * hd] = (acc_ref[g, :, :hd] / acc_ref[g, :, hd:]).astype(BF16)


def _flash(q, k, v, *, n_batch, tb, c_len):
    m = q.shape[0]
    hd = AT_HEAD_DIM
    gw = AT_GROUP * hd
    tq = _pick(tb, FLASH_TQ)
    tk = _pick(tb, FLASH_TK)
    nq, nk = tb // tq, tb // tk
    kern = functools.partial(_flash_kernel, tq=tq, tk=tk, c_len=c_len, nk=nk)
    return pl.pallas_call(
        kern, grid=(n_batch, AT_KV_HEADS, nq, nk),
        in_specs=[pl.BlockSpec((tq, gw), lambda b, h, i, j: (b * nq + i, h)),
                  pl.BlockSpec((tk, hd), lambda b, h, i, j: (b * nk + j, h)),
                  pl.BlockSpec((tk, hd), lambda b, h, i, j: (b * nk + j, h))],
        out_specs=pl.BlockSpec((tq, gw), lambda b, h, i, j: (b * nq + i, h)),
        out_shape=jax.ShapeDtypeStruct((m, AT_WIDTH), BF16),
        scratch_shapes=[pltpu.VMEM((AT_GROUP, tq, 1), F32), pltpu.VMEM((AT_GROUP, tq, 2 * hd), F32),
                        pltpu.VMEM((2, tq, tk), F32), pltpu.VMEM((2, tq, tk), BF16)],
        compiler_params=_params(4), name="flash_attention",
    )(q, k, v)


def _merge_kernel(yr, ys, ya, g0, g1, g2, wr, ws, wa, o_ref):
    m = _sigmoid(g0[...]) * jnp.dot(yr[...], wr[...], preferred_element_type=F32)
    m = m + _sigmoid(g1[...]) * jnp.dot(ys[...], ws[...], preferred_element_type=F32)
    m = m + _sigmoid(g2[...]) * jnp.dot(ya[...], wa[...], preferred_element_type=F32)
    o_ref[...] = m.astype(BF16)


def _merge(y_rw, y_s5, y_at, p, w_branch, *, tb):
    m = p.shape[0]
    d = w_branch.shape[1]
    tm = _pick(tb, (1056, 1024, 640, 512, 256, 128))
    tn = 512
    gcol = P_GATE // tn
    nd = d // tn
    gate = lambda br: pl.BlockSpec((tm, tn), lambda i, n: (i, gcol + br * nd + n))
    return pl.pallas_call(
        _merge_kernel, grid=(m // tm, nd),
        in_specs=[pl.BlockSpec((tm, RW_WIDTH), lambda i, n: (i, 0)),
                  pl.BlockSpec((tm, S5_WIDTH), lambda i, n: (i, 0)),
                  pl.BlockSpec((tm, AT_WIDTH), lambda i, n: (i, 0)),
                  gate(0), gate(1), gate(2),
                  pl.BlockSpec((RW_WIDTH, tn), lambda i, n: (0, n)),
                  pl.BlockSpec((S5_WIDTH, tn), lambda i, n: (1, n)),
                  pl.BlockSpec((AT_WIDTH, tn), lambda i, n: (1, n))],
        out_specs=pl.BlockSpec((tm, tn), lambda i, n: (i, n)),
        out_shape=jax.ShapeDtypeStruct((m, d), BF16),
        compiler_params=_params(2), name="merge",
    )(y_rw, y_s5, y_at, p, p, p, w_branch, w_branch, w_branch)


def _outproj_kernel(m_ref, w_ref, x_ref, mod_ref, o_ref, *, tpb, tm, c_len, n_batch):
    i = pl.program_id(0)
    is_ctx = _row_is_ctx(i, tpb, tm, c_len)
    gate = jnp.where(is_ctx, mod_ref[n_batch:n_batch + 1, :], mod_ref[pl.ds(i // tpb, 1), :])
    o_ref[...] = x_ref[...] + gate * jnp.dot(m_ref[...], w_ref[...], preferred_element_type=F32)


def _outproj(mm, w_out, x, mod, *, tb, c_len, n_batch):
    m, d = x.shape
    tm = _pick(tb, (1056, 1024, 640, 512, 256, 128))
    tn = 512
    kern = functools.partial(_outproj_kernel, tpb=tb // tm, tm=tm, c_len=c_len, n_batch=n_batch)
    return pl.pallas_call(
        kern, grid=(m // tm, d // tn),
        in_specs=[pl.BlockSpec((tm, d), lambda i, n: (i, 0)),
                  pl.BlockSpec((d, tn), lambda i, n: (0, n)),
                  pl.BlockSpec((tm, tn), lambda i, n: (i, n)),
                  pl.BlockSpec((mod.shape[0], tn), lambda i, n: (0, 2 * d // tn + n))],
        out_specs=pl.BlockSpec((tm, tn), lambda i, n: (i, n)),
        out_shape=jax.ShapeDtypeStruct((m, d), F32),
        compiler_params=_params(2), name="outproj",
    )(mm, w_out, x, mod)


def _router_kernel(x_ref, mod_ref, g_ref, rt_ref, h_ref, aff_ref, *, tpb, tm, c_len, n_batch, d):
    i = pl.program_id(0)
    is_ctx = _row_is_ctx(i, tpb, tm, c_len)
    b = i // tpb
    shift = _mod_rows(mod_ref, b, n_batch, 3, d, is_ctx)
    scale = _mod_rows(mod_ref, b, n_batch, 4, d, is_ctx)
    h = _rms(x_ref[...]) * g_ref[...]
    h = h * (1 + scale) + shift
    h_ref[...] = h
    logits = _dot_f32(rt_ref[...], h, dims=(((1,), (1,)), ((), ())))
    e = jnp.exp(logits - jnp.max(logits, axis=0, keepdims=True))
    aff_ref[...] = e / jnp.sum(e, axis=0, keepdims=True)


def _router(x, mod, g, router_t, *, tb, c_len, n_batch):
    m, d = x.shape
    ne = router_t.shape[0]
    tm = _pick(tb, (768, 640, 512, 384, 256, 128))
    kern = functools.partial(_router_kernel, tpb=tb // tm, tm=tm, c_len=c_len, n_batch=n_batch, d=d)
    return pl.pallas_call(
        kern, grid=(m // tm,),
        in_specs=[pl.BlockSpec((tm, d), lambda i: (i, 0)),
                  pl.BlockSpec(mod.shape, lambda i: (0, 0)),
                  pl.BlockSpec((1, d), lambda i: (0, 0)),
                  pl.BlockSpec((ne, d), lambda i: (0, 0))],
        out_specs=[pl.BlockSpec((tm, d), lambda i: (i, 0)), pl.BlockSpec((ne, tm), lambda i: (0, i))],
        out_shape=[jax.ShapeDtypeStruct((m, d), F32), jax.ShapeDtypeStruct((ne, m), F32)],
        compiler_params=_params(1), name="moe_router",
    )(x, mod, g.reshape(1, d), router_t)


def _select_kernel(aff_ref, tri_ref, idx_ref, csum_ref, *, tb, c_len, caps):
    bits = pltpu.bitcast(aff_ref[...], jnp.int32)
    ne = bits.shape[0]
    lane = lax.broadcasted_iota(jnp.int32, bits.shape, 1)
    tri = tri_ref[...]
    for (lo, hi), cap, out_off in zip(((0, c_len), (c_len, tb)), caps, (caps[1], 0)):
        vals = jnp.where((lane >= lo) & (lane < hi), bits, -1)

        def refine(it, thr, vals=vals, cap=cap):
            cand = thr | lax.shift_left(jnp.int32(1), 30 - it)
            cnt = jnp.sum(jnp.where(vals >= cand, 1.0, 0.0), axis=-1, keepdims=True)
            return jnp.where(cnt >= cap, cand, thr)

        thr = lax.fori_loop(0, 31, refine, jnp.zeros((ne, 1), jnp.int32))
        gt = vals > thr
        need = cap - jnp.sum(jnp.where(gt, 1.0, 0.0), axis=-1, keepdims=True)
        eq = jnp.where(vals == thr, 1.0, 0.0)
        run_eq = jnp.zeros((ne, 1), F32)
        run_sel = jnp.zeros((ne, 1), F32)
        for j in range(lo // 128, hi // 128):
            blk = eq[:, j * 128:(j + 1) * 128]
            pre = jnp.dot(blk.astype(BF16), tri, preferred_element_type=F32) + run_eq
            run_eq = run_eq + jnp.sum(blk, axis=-1, keepdims=True)
            sel = jnp.where(gt[:, j * 128:(j + 1) * 128] | ((blk > 0) & (pre <= need)), 1.0, 0.0)
            csum_ref[:, j * 128:(j + 1) * 128] = jnp.dot(sel.astype(BF16), tri, preferred_element_type=F32) + run_sel
            run_sel = run_sel + jnp.sum(sel, axis=-1, keepdims=True)

        n_seg = hi - lo
        ct = min(n_seg, 1024)
        slot = lax.broadcasted_iota(jnp.int32, (cap, 1), 0).astype(F32)
        ones = jnp.ones((ct, 128), BF16)

        def per_expert(e, carry, lo=lo, n_seg=n_seg, ct=ct, cap=cap, out_off=out_off, slot=slot, ones=ones):
            cnt = jnp.zeros((cap, 128), F32)
            for c0 in range(0, n_seg, ct):
                row = csum_ref[pl.ds(e, 1), lo + c0:lo + c0 + ct]
                cnt = cnt + jnp.dot(jnp.where(row <= slot, 1.0, 0.0).astype(BF16), ones, preferred_element_type=F32)
            idx_ref[e, 0, out_off:out_off + cap, :] = cnt.astype(jnp.int32)
            return carry

        lax.fori_loop(0, ne, per_expert, 0)


def _select(aff_t, tri, *, n_batch, tb, c_len, caps):
    ne, m = aff_t.shape
    cap_tot = caps[0] + caps[1]
    return pl.pallas_call(
        functools.partial(_select_kernel, tb=tb, c_len=c_len, caps=caps), grid=(n_batch,),
        in_specs=[pl.BlockSpec((ne, tb), lambda b: (0, b)), pl.BlockSpec(tri.shape, lambda b: (0, 0))],
        out_specs=pl.BlockSpec((ne, 1, cap_tot, 128), lambda b: (0, b, 0, 0)),
        out_shape=jax.ShapeDtypeStruct((ne, n_batch, cap_tot, 128), jnp.int32),
        scratch_shapes=[pltpu.VMEM((ne, tb), F32)],
        compiler_params=_params(1), name="moe_select",
    )(aff_t, tri)


MOE_FF_CHUNK = 256
MOE_DMA_UNROLL = 8


def _moe_kernel(idx_ref, gate_ref, mod_ref, wg_ref, wu_ref, wd_ref, h_hbm, x_in, x_hbm, ys, xb, xrow, sem,
                *, rows, n_batch, cap_lat, d, n_ff):
    del x_in
    e, b, f = pl.program_id(0), pl.program_id(1), pl.program_id(2)
    base = (e * n_batch + b) * rows

    def gather_h(s):
        return pltpu.make_async_copy(h_hbm.at[pl.ds(idx_ref[base + s], 1), :], ys.at[pl.ds(s, 1), :], sem.at[0])

    def gather_x(s):
        return pltpu.make_async_copy(x_hbm.at[pl.ds(idx_ref[base + s], 1), :], xrow.at[pl.ds(s, 1), :], sem.at[1])

    def scatter_x(s):
        return pltpu.make_async_copy(xrow.at[pl.ds(s, 1), :], x_hbm.at[pl.ds(idx_ref[base + s], 1), :], sem.at[2])

    def each_row(fn):
        def body(s, c):
            fn(s)
            return c
        lax.fori_loop(0, rows, body, 0, unroll=MOE_DMA_UNROLL)

    @pl.when(f == 0)
    def _():
        each_row(lambda s: (gather_h(s).start(), gather_x(s).start()))
        each_row(lambda s: gather_h(s).wait())
        xb[...] = ys[...].astype(BF16)
        ys[...] = jnp.zeros_like(ys)

    x = xb[...]
    hid = (_silu(jnp.dot(x, wg_ref[0, 0].astype(BF16), preferred_element_type=F32))
           * jnp.dot(x, wu_ref[0, 0].astype(BF16), preferred_element_type=F32))
    ys[...] += jnp.dot(hid.astype(BF16), wd_ref[0, 0].astype(BF16), preferred_element_type=F32)

    @pl.when(f == n_ff - 1)
    def _():
        each_row(lambda s: gather_x(s).wait())
        is_ctx = lax.broadcasted_iota(jnp.int32, (rows, 1), 0) >= cap_lat
        g_mlp = _mod_rows(mod_ref, b, n_batch, 5, d, is_ctx)
        gate = gate_ref[0]
        for j in range(d // 128):
            sl = slice(j * 128, (j + 1) * 128)
            xrow[:, sl] = xrow[:, sl] + g_mlp[:, sl] * (ys[:, sl] * gate)
        each_row(lambda s: scatter_x(s).start())
        each_row(lambda s: scatter_x(s).wait())


def _moe(idx, gate_b, mod, wg, wu, wd, layer, h2, x, *, n_batch, rows, cap_lat):
    m, d = x.shape
    _, ne, _, ff = wg.shape
    fc = MOE_FF_CHUNK
    n_ff = ff // fc
    kern = functools.partial(_moe_kernel, rows=rows, n_batch=n_batch, cap_lat=cap_lat, d=d, n_ff=n_ff)
    grid_spec = pltpu.PrefetchScalarGridSpec(
        num_scalar_prefetch=1, grid=(ne, n_batch, n_ff),
        in_specs=[pl.BlockSpec((1, rows, 128), lambda e, b, f, idx: (e * n_batch + b, 0, 0)),
                  pl.BlockSpec(mod.shape, lambda e, b, f, idx: (0, 0)),
                  pl.BlockSpec((1, 1, d, fc), lambda e, b, f, idx: (layer, e, 0, f)),
                  pl.BlockSpec((1, 1, d, fc), lambda e, b, f, idx: (layer, e, 0, f)),
                  pl.BlockSpec((1, 1, fc, d), lambda e, b, f, idx: (layer, e, f, 0)),
                  pl.BlockSpec(memory_space=pl.ANY),
                  pl.BlockSpec(memory_space=pl.ANY)],
        out_specs=pl.BlockSpec(memory_space=pl.ANY),
        scratch_shapes=[pltpu.VMEM((rows, d), F32), pltpu.VMEM((rows, d), BF16), pltpu.VMEM((rows, d), F32),
                        pltpu.SemaphoreType.DMA((3,))])
    return pl.pallas_call(
        kern, grid_spec=grid_spec, out_shape=jax.ShapeDtypeStruct((m, d), F32),
        input_output_aliases={7: 0},
        compiler_params=pltpu.CompilerParams(dimension_semantics=("arbitrary",) * 3, vmem_limit_bytes=VMEM_LIMIT,
                                             has_side_effects=True),
        name="moe_experts",
    )(idx, gate_b, mod, wg, wu, wd, h2, x)


def _final_kernel(x_ref, g_ref, o_ref):
    o_ref[0] = _rms(x_ref[0]) * g_ref[...]


def _final_norm(x3, g, *, c_len, seq):
    n_batch, _, d = x3.shape
    tr = _pick(math.gcd(c_len, seq), (256, 128, 64, 32, 16, 8))
    off = c_len // tr
    return pl.pallas_call(
        _final_kernel, grid=(n_batch, seq // tr),
        in_specs=[pl.BlockSpec((1, tr, d), lambda b, i: (b, i + off, 0)), pl.BlockSpec((1, d), lambda b, i: (0, 0))],
        out_specs=pl.BlockSpec((1, tr, d), lambda b, i: (b, i, 0)),
        out_shape=jax.ShapeDtypeStruct((n_batch, seq, d), F32),
        compiler_params=_params(2), name="final_norm",
    )(x3, g.reshape(1, d))


def _block_ones(n, blk):
    i = jnp.arange(n)
    return (i[:, None] // blk == i[None, :] // blk).astype(BF16)


def _rope_tables(seq, c_len):
    rows = seq // GRID_W
    row = jnp.repeat(jnp.arange(rows, dtype=F32), GRID_W)
    col = jnp.tile(jnp.arange(GRID_W, dtype=F32), rows)
    quarter = AT_HEAD_DIM // 4
    freqs = ROPE_THETA ** (-jnp.arange(quarter, dtype=F32) / quarter)
    ang_r, ang_c = row[:, None] * freqs, col[:, None] * freqs
    cos = jnp.concatenate([jnp.cos(ang_r)] * 2 + [jnp.cos(ang_c)] * 2, axis=1)
    sin = jnp.concatenate([-jnp.sin(ang_r), jnp.sin(ang_r), -jnp.sin(ang_c), jnp.sin(ang_c)], axis=1)
    cos = jnp.concatenate([jnp.ones((c_len, AT_HEAD_DIM), F32), cos], axis=0)
    sin = jnp.concatenate([jnp.zeros((c_len, AT_HEAD_DIM), F32), sin], axis=0)
    return cos, sin


def kernel(x, c, ctx, c_ctx, mod_w, mod_b, norm1_g, norm2_g, w_in, rwkv_conv, rwkv_w0, rwkv_w2, rwkv_a0, rwkv_a2, rwkv_g2, rwkv_kk, rwkv_ka, rwkv_rk, rwkv_ln_g, rwkv_ln_b, s5_lam_re, s5_lam_im, s5_log_dt, s5_b_re, s5_b_im, s5_c_re, s5_c_im, s5_d, s5_glu, attn_qn, attn_kn, w_branch, w_out, router, exp_gate, exp_up, exp_down, final_g):
    n_batch, seq, d = x.shape
    c_len = ctx.shape[1]
    depth = mod_w.shape[0]
    tb = c_len + seq
    m = n_batch * tb
    tc = math.gcd(c_len, 256)
    ne = router.shape[2]
    cap_ctx, cap_lat = CAPACITY_FACTOR * c_len // ne, CAPACITY_FACTOR * seq // ne
    rows = cap_ctx + cap_lat

    xa = jnp.concatenate([ctx, x], axis=1).reshape(m, d)
    c_all_t = jnp.concatenate([c, c_ctx[None]], axis=0).T
    mods = _modulation(c_all_t, mod_w, mod_b)
    ones_head = _block_ones(RW_WIDTH, RW_HEAD_DIM)
    tri = (jnp.arange(128)[:, None] <= jnp.arange(128)[None, :]).astype(BF16)
    cos_t, sin_t = _rope_tables(seq, c_len)
    s5_perms = _s5_perms(tc)
    row_base = (jnp.arange(n_batch, dtype=jnp.int32) * tb)[None, :, None]
    seg_base = jnp.concatenate([jnp.full((cap_lat,), c_len, jnp.int32), jnp.zeros((cap_ctx,), jnp.int32)])[None, None, :]

    for l in range(depth):
        mod = mods[l]
        p = _inproj(xa, mod, norm1_g[l], w_in, l, tb=tb, c_len=c_len, n_batch=n_batch)

        r, v, kk, w0, kd0, b0, w1, kd1, b1, bonus, g = _rwkv_prep(
            p, rwkv_conv[l], rwkv_w0[l], rwkv_w2[l], rwkv_a0[l], rwkv_a2[l], rwkv_g2[l], rwkv_kk[l],
            rwkv_ka[l], rwkv_rk[l].reshape(-1), ones_head, tb=tb, c_len=c_len)
        o0, o1 = _rwkv_scan(r, v, kk, w0, kd0, b0, w1, kd1, b1, n_batch=n_batch, tb=tb, c_len=c_len)
        y_rw = _rwkv_finish(o0, o1, bonus, g, rwkv_ln_g[l], rwkv_ln_b[l], ones_head, tb=tb)

        a_rows, pws, bres, bims, cres, cims = [], [], [], [], [], []
        for z in range(2):
            pws += _s5_powers(s5_lam_re[l, z], s5_lam_im[l, z], s5_log_dt[l, z], tc // S5_SEGMENTS)
            ab_re, ab_im, bb_re, bb_im = _s5_discretize(s5_lam_re[l, z], s5_lam_im[l, z], s5_log_dt[l, z],
                                                        s5_b_re[l, z], s5_b_im[l, z])
            a_rows += [ab_re.reshape(1, S5_NX), ab_im.reshape(1, S5_NX)]
            bres.append(_s5_pack_in(bb_re))
            bims.append(_s5_pack_in(bb_im))
            cres.append(_s5_pack_out(s5_c_re[l, z]))
            cims.append(_s5_pack_out(s5_c_im[l, z]))
        stack = lambda ts: jnp.stack(ts).astype(BF16)
        ys0, ys1 = _s5_scan(p, jnp.concatenate(a_rows, axis=0), jnp.stack(pws), s5_perms, stack(bres), stack(bims), stack(cres), stack(cims),
                            n_batch=n_batch, tb=tb, c_len=c_len, tc=tc)
        y_s5 = _s5_finish(ys0, ys1, p, s5_d[l], s5_glu[l].astype(BF16), tb=tb)

        q_r, k_r, v_b = _attn_prep(p, cos_t, sin_t, attn_qn[l], attn_kn[l], tb=tb)
        y_at = _flash(q_r, k_r, v_b, n_batch=n_batch, tb=tb, c_len=c_len)

        mm = _merge(y_rw, y_s5, y_at, p, w_branch[l].astype(BF16), tb=tb)
        x1 = _outproj(mm, w_out[l].astype(BF16), xa, mod, tb=tb, c_len=c_len, n_batch=n_batch)

        h2, aff_t = _router(x1, mod, norm2_g[l], router[l].T, tb=tb, c_len=c_len, n_batch=n_batch)
        pos = _select(aff_t, tri, n_batch=n_batch, tb=tb, c_len=c_len, caps=(cap_ctx, cap_lat))[..., 0]
        idx = pos + seg_base + row_base
        gate = jnp.take_along_axis(aff_t, idx.reshape(ne, -1), axis=1)
        gate_b = jnp.broadcast_to(gate.reshape(ne * n_batch, rows, 1), (ne * n_batch, rows, 128))
        xa = _moe(idx.reshape(-1), gate_b, mod, exp_gate, exp_up, exp_down, l, h2, x1,
                  n_batch=n_batch, rows=rows, cap_lat=cap_lat)

    return _final_norm(xa.reshape(n_batch, tb, d), final_g, c_len=c_len, seq=seq)
```

```python
import functools
import math

import jax
import jax.numpy as jnp
from jax import lax
from jax.experimental import pallas as pl
from jax.experimental.pallas import tpu as pltpu

F32 = jnp.float32
BF16 = jnp.bfloat16

NORM_EPS = 1e-6
N_MOD = 6
GRID_W = 64

RW_HEADS = 8
RW_HEAD_DIM = 64
RW_WIDTH = 512
RW_RANK = 64
RW_GATE_RANK = 128
RW_LN_EPS = 64e-5

S5_WIDTH = 512
S5_GROUP = 16
S5_GROUPS = 32
S5_STATE = 64
S5_NX = S5_GROUPS * S5_STATE
S5_PACKS = 4

AT_HEADS = 8
AT_KV_HEADS = 2
AT_GROUP = AT_HEADS // AT_KV_HEADS
AT_HEAD_DIM = 128
AT_WIDTH = 1024
AT_KV_WIDTH = 256
ROPE_THETA = 10000.0

N_EXPERTS = 16
CAPACITY_FACTOR = 2

OFF_WD = 3 * RW_WIDTH
OFF_S5 = OFF_WD + 4 * RW_RANK + RW_GATE_RANK
OFF_Q = OFF_S5 + S5_WIDTH
OFF_K = OFF_Q + AT_WIDTH
OFF_GATE = OFF_K + 2 * AT_KV_WIDTH

P_RW = 0
P_S5 = 2048
P_KV = 2560
P_Q = 3072
P_GATE = 4096

VMEM_LIMIT = 56 * 1024 * 1024


def _params(n_axes, vmem=VMEM_LIMIT):
    return pltpu.CompilerParams(dimension_semantics=("arbitrary",) * n_axes, vmem_limit_bytes=vmem)


def _pick(n, cands):
    for t in cands:
        if n % t == 0:
            return t
    raise ValueError(f"no tile for {n}")


def _split3(x):
    hi = x.astype(BF16)
    r1 = x - hi.astype(F32)
    mid = r1.astype(BF16)
    lo = (r1 - mid.astype(F32)).astype(BF16)
    return hi, mid, lo


def _dot_exact_rhs(x, w_bf16, pieces=3):
    parts = _split3(x)[:pieces]
    acc = jnp.dot(parts[0], w_bf16, preferred_element_type=F32)
    for p in parts[1:]:
        acc = acc + jnp.dot(p, w_bf16, preferred_element_type=F32)
    return acc


def _dot_f32(a, b, dims=(((1,), (0,)), ((), ()))):
    a1, a2, a3 = _split3(a)
    b1, b2, b3 = _split3(b)
    d = functools.partial(lax.dot_general, dimension_numbers=dims, preferred_element_type=F32)
    return (d(a1, b1) + (d(a1, b2) + d(a2, b1))) + ((d(a1, b3) + d(a3, b1)) + d(a2, b2))


def _sigmoid(x):
    return 1.0 / (1.0 + jnp.exp(-x))


def _silu(x):
    return x * _sigmoid(x)


def _row_is_ctx(i, tiles_per_batch, tm, c_len):
    pos = (i % tiles_per_batch) * tm + lax.broadcasted_iota(jnp.int32, (tm, 1), 0)
    return pos < c_len


def _mod_rows(mod_ref, b, n_batch, which, d, is_ctx):
    lat = mod_ref[pl.ds(b, 1), which * d:(which + 1) * d]
    ctx = mod_ref[n_batch:n_batch + 1, which * d:(which + 1) * d]
    return jnp.where(is_ctx, ctx, lat)


def _rms(x):
    return x * lax.rsqrt(jnp.mean(x * x, axis=-1, keepdims=True) + NORM_EPS)


def _mod_kernel(ct_ref, w_ref, b_ref, o_ref, *, n_rows):
    w = w_ref[0]
    for m in range(n_rows):
        col = ct_ref[:, m:m + 1]
        o_ref[0, m:m + 1, :] = jnp.sum(w * _silu(col), axis=0, keepdims=True) + b_ref[0]


def _modulation(c_all_t, mod_w, mod_b):
    depth, d, n6 = mod_w.shape
    n_rows = c_all_t.shape[1]
    tn = 512
    return pl.pallas_call(
        functools.partial(_mod_kernel, n_rows=n_rows),
        grid=(depth, n6 // tn),
        in_specs=[pl.BlockSpec((d, n_rows), lambda l, n: (0, 0)),
                  pl.BlockSpec((1, d, tn), lambda l, n: (l, 0, n)),
                  pl.BlockSpec((1, 1, tn), lambda l, n: (l, 0, n))],
        out_specs=pl.BlockSpec((1, n_rows, tn), lambda l, n: (l, 0, n)),
        out_shape=jax.ShapeDtypeStruct((depth, n_rows, n6), F32),
        compiler_params=_params(2), name="modulation",
    )(c_all_t, mod_w, mod_b.reshape(depth, 1, n6))


def _inproj_kernel(x_ref, mod_ref, g_ref, w_ref, o_ref, h_ref, *, tpb, tm, c_len, n_batch, d):
    i = pl.program_id(0)

    @pl.when(pl.program_id(1) == 0)
    def _():
        b = i // tpb
        n_chunks = next(n for n in (4, 3, 2, 1) if tm % (16 * n) == 0)
        rc = tm // n_chunks
        for j in range(n_chunks):
            rows = slice(j * rc, (j + 1) * rc)
            is_ctx = _row_is_ctx(i, tpb, tm, c_len)[rows]
            shift = _mod_rows(mod_ref, b, n_batch, 0, d, is_ctx)
            scale = _mod_rows(mod_ref, b, n_batch, 1, d, is_ctx)
            h = _rms(x_ref[rows, :]) * g_ref[...]
            h_ref[rows, :] = (h * (1 + scale) + shift).astype(BF16)

    o_ref[...] = jnp.dot(h_ref[...], w_ref[0].astype(BF16), preferred_element_type=F32)


INPROJ_TN = 512


def _w_in_offset(n):
    u = 128
    t = INPROJ_TN // u
    units = jnp.where(n < P_S5 // INPROJ_TN, n * t,
            jnp.where(n < P_KV // INPROJ_TN, OFF_S5 // u + (n - P_S5 // INPROJ_TN) * t,
            jnp.where(n < P_Q // INPROJ_TN, OFF_K // u + (n - P_KV // INPROJ_TN) * t,
            jnp.where(n < P_GATE // INPROJ_TN, OFF_Q // u + (n - P_Q // INPROJ_TN) * t,
                      OFF_GATE // u + (n - P_GATE // INPROJ_TN) * t))))
    return units * u


def _inproj(x, mod, g, w_in, layer, *, tb, c_len, n_batch):
    m, d = x.shape
    n_out = P_GATE + w_in.shape[2] - OFF_GATE
    tm = _pick(tb, (1056, 1024, 768, 640, 512, 256, 128))
    tn = INPROJ_TN
    kern = functools.partial(_inproj_kernel, tpb=tb // tm, tm=tm, c_len=c_len, n_batch=n_batch, d=d)
    return pl.pallas_call(
        kern, grid=(m // tm, n_out // tn),
        in_specs=[pl.BlockSpec((tm, d), lambda i, n: (i, 0)),
                  pl.BlockSpec(mod.shape, lambda i, n: (0, 0)),
                  pl.BlockSpec((1, d), lambda i, n: (0, 0)),
                  pl.BlockSpec((pl.Element(1), pl.Element(d), pl.Element(tn)), lambda i, n: (layer, 0, _w_in_offset(n)))],
        out_specs=pl.BlockSpec((tm, tn), lambda i, n: (i, n)),
        out_shape=jax.ShapeDtypeStruct((m, n_out), F32),
        scratch_shapes=[pltpu.VMEM((tm, d), BF16)],
        compiler_params=_params(2), name="inproj",
    )(x, mod, g.reshape(1, d), w_in)


def _softplus(z):
    return jnp.maximum(z, 0.0) + jnp.log(1.0 + jnp.exp(-jnp.abs(z)))


def _rwkv_prep_kernel(p_ref, prev_ref, next_ref, conv_ref, w0_ref, w2_ref, a0_ref, a2_ref, g2_ref,
                      kk_ref, ka_ref, rk_ref, ones_ref,
                      r_o, v_o, kk_o, w0_o, kd0_o, b0_o, w1_o, kd1_o, b1_o, bonus_o, g_o,
                      *, tr, tpb, c_len, tb):
    i = pl.program_id(0)
    c = RW_WIDTH
    pos = (i % tpb) * tr + lax.broadcasted_iota(jnp.int32, (tr, 1), 0)
    row = lax.broadcasted_iota(jnp.int32, (tr, 1), 0)
    first = (pos == 0) | (pos == c_len)
    last = (pos == c_len - 1) | (pos == tb - 1)

    x = p_ref[:, 0:3 * c]
    x_prev = jnp.where(row == 0, prev_ref[7:8, 0:3 * c], pltpu.roll(x, 1, 0))
    x_next = jnp.where(row == tr - 1, next_ref[0:1, 0:3 * c], pltpu.roll(x, tr - 1, 0))
    x_prev = jnp.where(first, 0.0, x_prev)
    x_next = jnp.where(last, 0.0, x_next)
    rkv = x_prev * conv_ref[0:1, :] + x * conv_ref[1:2, :] + x_next * conv_ref[2:3, :]
    r, k, v = rkv[:, 0:c], rkv[:, c:2 * c], rkv[:, 2 * c:3 * c]
    ones = ones_ref[...]

    kk = k * kk_ref[...]
    kk = kk * lax.rsqrt(_dot_exact_rhs(kk * kk, ones) + 1e-12)
    r_o[...] = r
    v_o[...] = v
    kk_o[...] = kk

    kd_sum = jnp.zeros_like(k)
    for z, (w_o, kd_o, b_o) in enumerate(((w0_o, kd0_o, b0_o), (w1_o, kd1_o, b1_o))):
        wd = jnp.tanh(p_ref[:, OFF_WD + z * RW_RANK:OFF_WD + (z + 1) * RW_RANK])
        ad = p_ref[:, OFF_WD + (2 + z) * RW_RANK:OFF_WD + (3 + z) * RW_RANK]
        w_lin = w0_ref[z:z + 1, :] + _dot_f32(wd, w2_ref[z])
        w_log = -_softplus(-w_lin) - 0.5
        w_o[...] = -jnp.exp(w_log)
        a = _sigmoid(a0_ref[z:z + 1, :] + _dot_f32(ad, a2_ref[z]))
        kd = k * (1 + (a - 1) * ka_ref[...])
        kd_o[...] = kd
        b_o[...] = kk * a
        kd_sum = kd_sum + kd

    bonus_o[...] = _dot_exact_rhs(r * kd_sum * rk_ref[...], ones) * v
    gd = _sigmoid(p_ref[:, OFF_WD + 4 * RW_RANK:OFF_WD + 4 * RW_RANK + RW_GATE_RANK])
    g_o[...] = _dot_f32(gd, g2_ref[...])


def _rwkv_prep(p, conv_w, w0, w2, a0, a2, g2, k_k, k_a, r_k, ones_bd, *, tb, c_len):
    m = p.shape[0]
    c = RW_WIDTH
    tr = _pick(tb, (528, 512, 320, 256, 128))
    nblk8 = m // 8
    kern = functools.partial(_rwkv_prep_kernel, tr=tr, tpb=tb // tr, c_len=c_len, tb=tb)
    full = lambda a: pl.BlockSpec(a.shape, lambda i: (0,) * a.ndim)
    args = (conv_w, w0, w2, a0, a2, g2, k_k.reshape(1, c), k_a.reshape(1, c), r_k.reshape(1, c), ones_bd)
    out = jax.ShapeDtypeStruct((m, c), F32)
    return pl.pallas_call(
        kern, grid=(m // tr,),
        in_specs=[pl.BlockSpec((tr, 2048), lambda i: (i, 0)),
                  pl.BlockSpec((8, 2048), lambda i: (jnp.maximum(i * (tr // 8) - 1, 0), 0)),
                  pl.BlockSpec((8, 2048), lambda i: (jnp.minimum((i + 1) * (tr // 8), nblk8 - 1), 0))]
                 + [full(a) for a in args],
        out_specs=[pl.BlockSpec((tr, c), lambda i: (i, 0))] * 11,
        out_shape=[out] * 11,
        compiler_params=_params(1), name="rwkv_prep",
    )(p, p, p, *args)


def _rev_chunk(c, n_ctx_chunks, n_chunks):
    return jnp.where(c < n_ctx_chunks, n_ctx_chunks - 1 - c, n_chunks - 1 - (c - n_ctx_chunks))


RW_CHUNK = 64

_NN = (((1,), (0,)), ((), ()))
_NT = (((1,), (1,)), ((), ()))
_TN = (((0,), (0,)), ((), ()))


def _mm(a, b, dims=_NN, pa=1, pb=1):
    xs, ys = _split3(a)[:pa], _split3(b)[:pb]
    acc = None
    for i, x in enumerate(xs):
        for j, y in enumerate(ys):
            if i + j < max(pa, pb):
                d = lax.dot_general(x, y, dims, preferred_element_type=F32)
                acc = d if acc is None else acc + d
    return acc


_RW_PIECES = {"gram": (1, 1), "inverse": (2, 2), "values": (1, 1), "solve": (1, 1), "readout": (1, 1),
              "carry": (1, 1), "state": (2, 2)}


RW_NEWTON_STEPS = 1


def _rwkv_scan_kernel(r0, v0, k0, w0, d0, b0, r1, v1, k1, w1, d1, b1, o0, o1, h_ref):
    t = RW_CHUNK
    t2 = 2 * t

    @pl.when(pl.program_id(1) == 0)
    def _():
        h_ref[...] = jnp.zeros_like(h_ref)

    r2 = lax.broadcasted_iota(jnp.int32, (t2, t2), 0)
    c2 = lax.broadcasted_iota(jnp.int32, (t2, t2), 1)
    same = (r2 // t) == (c2 // t)
    eye = r2 == c2
    row = lax.broadcasted_iota(jnp.int32, (t, t), 0)
    col = lax.broadcasted_iota(jnp.int32, (t, t), 1)
    head0 = lax.broadcasted_iota(jnp.int32, (t, t2), 1) < RW_HEAD_DIM

    def stack(x):
        return jnp.concatenate([jnp.where(head0, x, 0.0), jnp.where(head0, 0.0, x)], axis=0)

    combos = []
    for z, (r, v, kk, w, kd, bb, o) in enumerate(((r0, v0, k0, w0, d0, b0, o0), (r1, v1, k1, w1, d1, b1, o1))):
        before = (c2 % t < r2 % t) if z == 0 else (c2 % t > r2 % t)
        strict = same & before
        incl = same & (before | eye)
        cum = jnp.where((col <= row) if z == 0 else (col >= row), 1.0, 0.0)
        lw_all = w[...]
        g_all = _mm(cum, lw_all, pb=3)
        gtot_all = g_all[t - 1:t, :] if z == 0 else g_all[0:1, :]
        for hp in range(RW_WIDTH // t2):
            sl = slice(hp * t2, (hp + 1) * t2)
            g = g_all[:, sl]
            e_inv = jnp.exp(-g)
            combos.append(dict(
                z=z, hp=hp, sl=sl, o=o, strict=strict, incl=incl, gtot=jnp.exp(gtot_all[:, sl]),
                a_st=stack(kk[:, sl] * jnp.exp(g - lw_all[:, sl])), r_st=stack(r[:, sl] * jnp.exp(g)),
                bt=bb[:, sl] * e_inv, kt=kd[:, sl] * e_inv, v_st=stack(v[:, sl])))

    for c in combos:
        mm = _mm(jnp.concatenate([c["a_st"], c["r_st"]], axis=0),
                 jnp.concatenate([c["bt"], c["bt"], c["kt"], c["kt"]], axis=0), _NT, *_RW_PIECES["gram"])
        c["n_pow"] = jnp.where(c["strict"], -mm[:t2, :t2], 0.0)
        c["l_ak"] = jnp.where(c["strict"], mm[:t2, t2:], 0.0)
        c["m_rb"] = jnp.where(c["incl"], mm[t2:, :t2], 0.0)
        c["m_rk"] = jnp.where(c["incl"], mm[t2:, t2:], 0.0)
        c["inv"] = jnp.where(eye, 1.0, 0.0) + c["n_pow"]
    for c in combos:
        c["y"] = _mm(jnp.concatenate([c["l_ak"], c["m_rk"]], axis=0), c["v_st"], _NN, *_RW_PIECES["values"])
    for c in combos:
        c["a_mat"] = jnp.where(eye, 1.0, 0.0) - c["n_pow"]
    for _ in range(5):
        for c in combos:
            c["n_pow"] = _mm(c["n_pow"], c["n_pow"])
        for c in combos:
            c["inv"] = c["inv"] + _mm(c["inv"], c["n_pow"])
    for _ in range(RW_NEWTON_STEPS):
        for c in combos:
            c["res"] = jnp.where(eye, 1.0, 0.0) - _mm(c["a_mat"], c["inv"], _NN, *_RW_PIECES["inverse"])
        for c in combos:
            c["inv"] = c["inv"] + _mm(c["inv"], c["res"])
    for c in combos:
        c["zz"] = _mm(c["inv"], jnp.concatenate([c["a_st"], c["y"][:t2]], axis=1), _NN, *_RW_PIECES["solve"])
    for c in combos:
        ro = _mm(c["m_rb"], c["zz"], _NN, *_RW_PIECES["readout"])
        c["r_hat"] = c["r_st"] - ro[:, :t2]
        c["o_hat"] = c["y"][t2:] - ro[:, t2:]
    for c in combos:
        btw = _mm(stack(c["bt"] * c["gtot"]), c["zz"], _TN, *_RW_PIECES["carry"])
        c["p"] = jnp.where(eye, c["gtot"], 0.0) - btw[:, :t2]
        c["q"] = _mm(stack(c["kt"] * c["gtot"]), c["v_st"], _TN, *_RW_PIECES["carry"]) - btw[:, t2:]
    for c in combos:
        h = h_ref[c["z"], c["hp"]]
        o_st = _mm(c["r_hat"], h, _NN, *_RW_PIECES["state"]) + c["o_hat"]
        c["o"][:, c["sl"]] = o_st[:t] + o_st[t:]
        h_ref[c["z"], c["hp"]] = _mm(c["p"], h, _NN, *_RW_PIECES["state"]) + c["q"]


def _rwkv_scan(r, v, kk, w0, kd0, b0, w1, kd1, b1, *, n_batch, tb, c_len):
    m, c = r.shape
    tc = RW_CHUNK
    nc, ncc = tb // tc, c_len // tc
    fwd = pl.BlockSpec((tc, c), lambda b, j: (b * nc + j, 0))
    rev = pl.BlockSpec((tc, c), lambda b, j: (b * nc + _rev_chunk(j, ncc, nc), 0))
    out = jax.ShapeDtypeStruct((m, c), F32)
    return pl.pallas_call(
        _rwkv_scan_kernel, grid=(n_batch, nc),
        in_specs=[fwd] * 6 + [rev] * 6,
        out_specs=[fwd, rev], out_shape=[out, out],
        scratch_shapes=[pltpu.VMEM((2, c // (2 * RW_HEAD_DIM), 2 * RW_HEAD_DIM, 2 * RW_HEAD_DIM), F32)],
        compiler_params=_params(2), name="rwkv_scan",
    )(r, v, kk, w0, kd0, b0, r, v, kk, w1, kd1, b1)


def _rwkv_finish_kernel(o0, o1, bonus, g, lng, lnb, ones_ref, y_ref):
    o = o0[...] + o1[...]
    ones = ones_ref[...]
    mu = _dot_exact_rhs(o, ones) * (1.0 / RW_HEAD_DIM)
    dev = o - mu
    var = _dot_exact_rhs(dev * dev, ones) * (1.0 / RW_HEAD_DIM)
    o_n = dev * lax.rsqrt(var + RW_LN_EPS)
    y_ref[...] = ((o_n * lng[...] + lnb[...] + bonus[...]) * g[...]).astype(BF16)


def _rwkv_finish(o0, o1, bonus, g, ln_g, ln_b, ones_bd, *, tb):
    m, c = o0.shape
    tr = _pick(tb, (1056, 1024, 640, 512, 256, 128))
    blk = pl.BlockSpec((tr, c), lambda i: (i, 0))
    vec = pl.BlockSpec((1, c), lambda i: (0, 0))
    return pl.pallas_call(
        _rwkv_finish_kernel, grid=(m // tr,),
        in_specs=[blk] * 4 + [vec, vec, pl.BlockSpec(ones_bd.shape, lambda i: (0, 0))],
        out_specs=blk, out_shape=jax.ShapeDtypeStruct((m, c), BF16),
        compiler_params=_params(1), name="rwkv_finish",
    )(o0, o1, bonus, g, ln_g.reshape(1, c), ln_b.reshape(1, c), ones_bd)


def _s5_discretize(lam_re, lam_im, log_dt, b_re, b_im):
    dt = jnp.exp(log_dt)[:, None]
    mag = jnp.exp(lam_re * dt)
    ab_re, ab_im = mag * jnp.cos(lam_im * dt), mag * jnp.sin(lam_im * dt)
    den = lam_re * lam_re + lam_im * lam_im
    num_re = ab_re - 1.0
    co_re = (num_re * lam_re + ab_im * lam_im) / den
    co_im = (ab_im * lam_re - num_re * lam_im) / den
    bb_re = co_re[..., None] * b_re - co_im[..., None] * b_im
    bb_im = co_re[..., None] * b_im + co_im[..., None] * b_re
    return ab_re, ab_im, bb_re, bb_im


def _s5_powers(lam_re, lam_im, log_dt, n):
    dt = jnp.exp(log_dt)[None, :, None]
    k = jnp.arange(1, n + 1, dtype=F32)[:, None, None]
    mag = jnp.exp(k * lam_re[None] * dt)
    ang = k * lam_im[None] * dt
    return [(mag * jnp.cos(ang)).reshape(n, S5_NX), (mag * jnp.sin(ang)).reshape(n, S5_NX)]


def _s5_pack_in(bb):
    gpp = S5_GROUPS // S5_PACKS
    bb = bb.reshape(S5_PACKS, gpp, S5_STATE, S5_GROUP)
    eye = jnp.eye(gpp, dtype=bb.dtype)
    t = jnp.einsum('qgnc,gh->qgchn', bb, eye)
    return t.reshape(S5_PACKS, gpp * S5_GROUP, gpp * S5_STATE)


def _s5_pack_out(cc):
    gpp = S5_GROUPS // S5_PACKS
    cc = cc.reshape(S5_PACKS, gpp, S5_GROUP, S5_STATE)
    eye = jnp.eye(gpp, dtype=cc.dtype)
    t = jnp.einsum('qgcn,gh->qgnhc', cc, eye)
    return t.reshape(S5_PACKS, gpp * S5_STATE, gpp * S5_GROUP)


S5_SEGMENTS = 8


def _cmul(ar, ai, br, bi):
    return ar * br - ai * bi, ar * bi + ai * br


def _s5_scan_kernel(u0, u1, a_ref, pw_ref, perm_ref, bre, bim, cre, cim, y0, y1, xr_ref, xi_ref, st_ref, *, tc):
    @pl.when(pl.program_id(1) == 0)
    def _():
        st_ref[...] = jnp.zeros_like(st_ref)

    cw = S5_WIDTH // S5_PACKS
    nw = S5_NX // S5_PACKS
    n_seg = S5_SEGMENTS
    seg = tc // n_seg
    sub = lax.broadcasted_iota(jnp.int32, (n_seg, S5_NX), 0)
    for z, (u, y) in enumerate(((u0, y0), (u1, y1))):
        perm = perm_ref[z]
        ub = jnp.dot(perm, u[...].astype(BF16), preferred_element_type=F32).astype(BF16)
        for q in range(S5_PACKS):
            uq = ub[:, q * cw:(q + 1) * cw]
            xr_ref[:, q * nw:(q + 1) * nw] = jnp.dot(uq, bre[z, q], preferred_element_type=F32)
            xi_ref[:, q * nw:(q + 1) * nw] = jnp.dot(uq, bim[z, q], preferred_element_type=F32)
        ar = a_ref[2 * z:2 * z + 1, :]
        ai = a_ref[2 * z + 1:2 * z + 2, :]

        sr = jnp.zeros((n_seg, S5_NX), F32)
        si = jnp.zeros((n_seg, S5_NX), F32)
        for i in range(seg):
            rows = slice(i * n_seg, (i + 1) * n_seg)
            pr, pi = _cmul(ar, ai, sr, si)
            sr, si = pr + xr_ref[rows, :], pi + xi_ref[rows, :]
            xr_ref[rows, :] = sr
            xi_ref[rows, :] = si

        a_seg_r = pw_ref[2 * z, seg - 1:seg, :]
        a_seg_i = pw_ref[2 * z + 1, seg - 1:seg, :]
        cr, ci = st_ref[2 * z:2 * z + 1, :], st_ref[2 * z + 1:2 * z + 2, :]
        cin_r = jnp.zeros((n_seg, S5_NX), F32)
        cin_i = jnp.zeros((n_seg, S5_NX), F32)
        for k in range(n_seg):
            sg = k if z == 0 else n_seg - 1 - k
            cin_r = jnp.where(sub == sg, cr, cin_r)
            cin_i = jnp.where(sub == sg, ci, cin_i)
            pr, pi = _cmul(a_seg_r, a_seg_i, cr, ci)
            cr, ci = pr + sr[sg:sg + 1, :], pi + si[sg:sg + 1, :]
        st_ref[2 * z:2 * z + 1, :] = cr
        st_ref[2 * z + 1:2 * z + 2, :] = ci

        for i in range(seg):
            rows = slice(i * n_seg, (i + 1) * n_seg)
            pr, pi = _cmul(pw_ref[2 * z, i:i + 1, :], pw_ref[2 * z + 1, i:i + 1, :], cin_r, cin_i)
            xr_ref[rows, :] = xr_ref[rows, :] + pr
            xi_ref[rows, :] = xi_ref[rows, :] + pi

        xr = xr_ref[...].astype(BF16)
        xi = xi_ref[...].astype(BF16)
        yp = jnp.concatenate(
            [jnp.dot(xr[:, q * nw:(q + 1) * nw], cre[z, q], preferred_element_type=F32)
             - jnp.dot(xi[:, q * nw:(q + 1) * nw], cim[z, q], preferred_element_type=F32) for q in range(S5_PACKS)], axis=1)
        y[...] = _mm(perm.astype(F32), yp, _TN, 1, 3)


def _s5_perms(tc):
    n_seg = S5_SEGMENTS
    seg = tc // n_seg
    i = jnp.arange(tc) // n_seg
    sg = jnp.arange(tc) % n_seg
    cols = jnp.stack([sg * seg + i, sg * seg + (seg - 1 - i)])
    return (cols[:, :, None] == jnp.arange(tc)[None, None, :]).astype(BF16)


def _s5_scan(p, a_all, pw, perms, bre, bim, cre, cim, *, n_batch, tb, c_len, tc):
    m = p.shape[0]
    nc, ncc = tb // tc, c_len // tc
    ucol = P_S5 // S5_WIDTH
    full = lambda a: pl.BlockSpec(a.shape, lambda b, j: (0,) * a.ndim)
    out = jax.ShapeDtypeStruct((m, S5_WIDTH), F32)
    return pl.pallas_call(
        functools.partial(_s5_scan_kernel, tc=tc), grid=(n_batch, nc),
        in_specs=[pl.BlockSpec((tc, S5_WIDTH), lambda b, j: (b * nc + j, ucol)),
                  pl.BlockSpec((tc, S5_WIDTH), lambda b, j: (b * nc + _rev_chunk(j, ncc, nc), ucol)),
                  full(a_all), full(pw), full(perms), full(bre), full(bim), full(cre), full(cim)],
        out_specs=[pl.BlockSpec((tc, S5_WIDTH), lambda b, j: (b * nc + j, 0)),
                   pl.BlockSpec((tc, S5_WIDTH), lambda b, j: (b * nc + _rev_chunk(j, ncc, nc), 0))],
        out_shape=[out, out],
        scratch_shapes=[pltpu.VMEM((tc, S5_NX), F32), pltpu.VMEM((tc, S5_NX), F32), pltpu.VMEM((4, S5_NX), F32)],
        compiler_params=_params(2), name="s5_scan",
    )(p, p, a_all, pw, perms, bre, bim, cre, cim)


def _s5_finish_kernel(y0, y1, u, d_ref, w_ref, o_ref):
    y = y0[...] + y1[...] + d_ref[...] * u[...]
    ge = 0.5 * y * (1.0 + jnp.tanh(math.sqrt(2.0 / math.pi) * (y + 0.044715 * (y * y * y))))
    gate = _sigmoid(jnp.dot(ge.astype(BF16), w_ref[...], preferred_element_type=F32))
    o_ref[...] = (ge * gate).astype(BF16)


def _s5_finish(y0, y1, p, d_skip, w_glu, *, tb):
    m, c = y0.shape
    tr = _pick(tb, (1056, 1024, 640, 512, 256, 128))
    blk = pl.BlockSpec((tr, c), lambda i: (i, 0))
    return pl.pallas_call(
        _s5_finish_kernel, grid=(m // tr,),
        in_specs=[blk, blk, pl.BlockSpec((tr, c), lambda i: (i, P_S5 // c)),
                  pl.BlockSpec((1, c), lambda i: (0, 0)), pl.BlockSpec((c, c), lambda i: (0, 0))],
        out_specs=blk, out_shape=jax.ShapeDtypeStruct((m, c), BF16),
        compiler_params=_params(1), name="s5_finish",
    )(y0, y1, p, d_skip.reshape(1, c), w_glu)


def _rope(x, cos, sin_signed):
    lane = lax.broadcasted_iota(jnp.int32, x.shape, 1)
    swapped = jnp.where(lane % 64 < 32, pltpu.roll(x, 96, 1), pltpu.roll(x, 32, 1))
    return x * cos + swapped * sin_signed


def _attn_prep_kernel(kv_ref, q_ref, cos_ref, sin_ref, qn_ref, kn_ref, qo, ko, vo):
    cos, sin = cos_ref[...], sin_ref[...]
    hd = AT_HEAD_DIM
    for h in range(AT_HEADS):
        q = _rms(q_ref[:, h * hd:(h + 1) * hd]) * qn_ref[...]
        qo[:, h * hd:(h + 1) * hd] = (_rope(q, cos, sin) * (hd ** -0.5 * math.log2(math.e))).astype(BF16)
    for h in range(AT_KV_HEADS):
        k = _rms(kv_ref[:, h * hd:(h + 1) * hd]) * kn_ref[...]
        ko[:, h * hd:(h + 1) * hd] = _rope(k, cos, sin).astype(BF16)
    vo[...] = kv_ref[:, AT_KV_WIDTH:2 * AT_KV_WIDTH].astype(BF16)


def _attn_prep(p, cos_t, sin_t, qn, kn, *, tb):
    m = p.shape[0]
    tr = _pick(tb, (528, 512, 320, 256, 128))
    tpb = tb // tr
    hd = AT_HEAD_DIM
    return pl.pallas_call(
        _attn_prep_kernel, grid=(m // tr,),
        in_specs=[pl.BlockSpec((tr, 2 * AT_KV_WIDTH), lambda i: (i, P_KV // (2 * AT_KV_WIDTH))),
                  pl.BlockSpec((tr, AT_WIDTH), lambda i: (i, P_Q // AT_WIDTH)),
                  pl.BlockSpec((tr, hd), lambda i: (i % tpb, 0)),
                  pl.BlockSpec((tr, hd), lambda i: (i % tpb, 0)),
                  pl.BlockSpec((1, hd), lambda i: (0, 0)), pl.BlockSpec((1, hd), lambda i: (0, 0))],
        out_specs=[pl.BlockSpec((tr, AT_WIDTH), lambda i: (i, 0)),
                   pl.BlockSpec((tr, AT_KV_WIDTH), lambda i: (i, 0)),
                   pl.BlockSpec((tr, AT_KV_WIDTH), lambda i: (i, 0))],
        out_shape=[jax.ShapeDtypeStruct((m, AT_WIDTH), BF16), jax.ShapeDtypeStruct((m, AT_KV_WIDTH), BF16),
                   jax.ShapeDtypeStruct((m, AT_KV_WIDTH), BF16)],
        compiler_params=_params(1), name="attn_prep",
    )(p, p, cos_t, sin_t, qn.reshape(1, hd), kn.reshape(1, hd))


FLASH_TQ = (1056, 1024, 640, 512, 256, 128)
FLASH_TK = (1408, 768, 640, 512, 384, 256, 128)


def _flash_kernel(q_ref, k_ref, v_ref, o_ref, m_ref, acc_ref, s_ref, p_ref, *, tq, tk, c_len, nk):
    qi, ki = pl.program_id(2), pl.program_id(3)
    hd = AT_HEAD_DIM

    @pl.when(ki == 0)
    def _():
        m_ref[...] = jnp.full_like(m_ref, -1e30)
        acc_ref[...] = jnp.zeros_like(acc_ref)

    def body(masked):
        k = k_ref[...]
        v_ext = jnp.concatenate([v_ref[...], jnp.ones((tk, hd), BF16)], axis=1)
        if masked:
            row = qi * tq + lax.broadcasted_iota(jnp.int32, (tq, 1), 0)
            col = ki * tk + lax.broadcasted_iota(jnp.int32, (1, tk), 1)
            hide = (row < c_len) & (col >= c_len)
        for g in range(AT_GROUP):
            b = g % 2
            s = lax.dot_general(q_ref[:, g * hd:(g + 1) * hd], k, _NT, preferred_element_type=F32)
            s_ref[b] = jnp.where(hide, -1e30, s) if masked else s
            m_prev = m_ref[g]
            m_new = jnp.maximum(m_prev, jnp.max(s_ref[b], axis=-1, keepdims=True))
            m_ref[g] = m_new
            p_ref[b] = jnp.exp2(s_ref[b] - m_new).astype(BF16)
            acc_ref[g] = jnp.exp2(m_prev - m_new) * acc_ref[g] + jnp.dot(p_ref[b], v_ext, preferred_element_type=F32)

    has_ctx = qi * tq < c_len
    pl.when(has_ctx)(lambda: body(True))
    pl.when(jnp.logical_not(has_ctx))(lambda: body(False))

    @pl.when(ki == nk - 1)
    def _():
        for g in range(AT_GROUP):
            o_ref[:, g * hd:(g + 1) * hd] = (acc_ref[g, :, :hd] / acc_ref[g, :, hd:]).astype(BF16)


def _flash(q, k, v, *, n_batch, tb, c_len):
    m = q.shape[0]
    hd = AT_HEAD_DIM
    gw = AT_GROUP * hd
    tq = _pick(tb, FLASH_TQ)
    tk = _pick(tb, FLASH_TK)
    nq, nk = tb // tq, tb // tk
    kern = functools.partial(_flash_kernel, tq=tq, tk=tk, c_len=c_len, nk=nk)
    return pl.pallas_call(
        kern, grid=(n_batch, AT_KV_HEADS, nq, nk),
        in_specs=[pl.BlockSpec((tq, gw), lambda b, h, i, j: (b * nq + i, h)),
                  pl.BlockSpec((tk, hd), lambda b, h, i, j: (b * nk + j, h)),
                  pl.BlockSpec((tk, hd), lambda b, h, i, j: (b * nk + j, h))],
        out_specs=pl.BlockSpec((tq, gw), lambda b, h, i, j: (b * nq + i, h)),
        out_shape=jax.ShapeDtypeStruct((m, AT_WIDTH), BF16),
        scratch_shapes=[pltpu.VMEM((AT_GROUP, tq, 1), F32), pltpu.VMEM((AT_GROUP, tq, 2 * hd), F32),
                        pltpu.VMEM((2, tq, tk), F32), pltpu.VMEM((2, tq, tk), BF16)],
        compiler_params=_params(4), name="flash_attention",
    )(q, k, v)


def _merge_kernel(yr, ys, ya, g0, g1, g2, wr, ws, wa, o_ref):
    m = _sigmoid(g0[...]) * jnp.dot(yr[...], wr[...], preferred_element_type=F32)
    m = m + _sigmoid(g1[...]) * jnp.dot(ys[...], ws[...], preferred_element_type=F32)
    m = m + _sigmoid(g2[...]) * jnp.dot(ya[...], wa[...], preferred_element_type=F32)
    o_ref[...] = m.astype(BF16)


def _merge(y_rw, y_s5, y_at, p, w_branch, *, tb):
    m = p.shape[0]
    d = w_branch.shape[1]
    tm = _pick(tb, (1056, 1024, 640, 512, 256, 128))
    tn = 512
    gcol = P_GATE // tn
    nd = d // tn
    gate = lambda br: pl.BlockSpec((tm, tn), lambda i, n: (i, gcol + br * nd + n))
    return pl.pallas_call(
        _merge_kernel, grid=(m // tm, nd),
        in_specs=[pl.BlockSpec((tm, RW_WIDTH), lambda i, n: (i, 0)),
                  pl.BlockSpec((tm, S5_WIDTH), lambda i, n: (i, 0)),
                  pl.BlockSpec((tm, AT_WIDTH), lambda i, n: (i, 0)),
                  gate(0), gate(1), gate(2),
                  pl.BlockSpec((RW_WIDTH, tn), lambda i, n: (0, n)),
                  pl.BlockSpec((S5_WIDTH, tn), lambda i, n: (1, n)),
                  pl.BlockSpec((AT_WIDTH, tn), lambda i, n: (1, n))],
        out_specs=pl.BlockSpec((tm, tn), lambda i, n: (i, n)),
        out_shape=jax.ShapeDtypeStruct((m, d), BF16),
        compiler_params=_params(2), name="merge",
    )(y_rw, y_s5, y_at, p, p, p, w_branch, w_branch, w_branch)


def _outproj_kernel(m_ref, w_ref, x_ref, mod_ref, o_ref, *, tpb, tm, c_len, n_batch):
    i = pl.program_id(0)
    is_ctx = _row_is_ctx(i, tpb, tm, c_len)
    gate = jnp.where(is_ctx, mod_ref[n_batch:n_batch + 1, :], mod_ref[pl.ds(i // tpb, 1), :])
    o_ref[...] = x_ref[...] + gate * jnp.dot(m_ref[...], w_ref[...], preferred_element_type=F32)


def _outproj(mm, w_out, x, mod, *, tb, c_len, n_batch):
    m, d = x.shape
    tm = _pick(tb, (1056, 1024, 640, 512, 256, 128))
    tn = 512
    kern = functools.partial(_outproj_kernel, tpb=tb // tm, tm=tm, c_len=c_len, n_batch=n_batch)
    return pl.pallas_call(
        kern, grid=(m // tm, d // tn),
        in_specs=[pl.BlockSpec((tm, d), lambda i, n: (i, 0)),
                  pl.BlockSpec((d, tn), lambda i, n: (0, n)),
                  pl.BlockSpec((tm, tn), lambda i, n: (i, n)),
                  pl.BlockSpec((mod.shape[0], tn), lambda i, n: (0, 2 * d // tn + n))],
        out_specs=pl.BlockSpec((tm, tn), lambda i, n: (i, n)),
        out_shape=jax.ShapeDtypeStruct((m, d), F32),
        compiler_params=_params(2), name="outproj",
    )(mm, w_out, x, mod)


def _router_kernel(x_ref, mod_ref, g_ref, rt_ref, h_ref, aff_ref, *, tpb, tm, c_len, n_batch, d):
    i = pl.program_id(0)
    is_ctx = _row_is_ctx(i, tpb, tm, c_len)
    b = i // tpb
    shift = _mod_rows(mod_ref, b, n_batch, 3, d, is_ctx)
    scale = _mod_rows(mod_ref, b, n_batch, 4, d, is_ctx)
    h = _rms(x_ref[...]) * g_ref[...]
    h = h * (1 + scale) + shift
    h_ref[...] = h
    logits = _dot_f32(rt_ref[...], h, dims=(((1,), (1,)), ((), ())))
    e = jnp.exp(logits - jnp.max(logits, axis=0, keepdims=True))
    aff_ref[...] = e / jnp.sum(e, axis=0, keepdims=True)


def _router(x, mod, g, router_t, *, tb, c_len, n_batch):
    m, d = x.shape
    ne = router_t.shape[0]
    tm = _pick(tb, (768, 640, 512, 384, 256, 128))
    kern = functools.partial(_router_kernel, tpb=tb // tm, tm=tm, c_len=c_len, n_batch=n_batch, d=d)
    return pl.pallas_call(
        kern, grid=(m // tm,),
        in_specs=[pl.BlockSpec((tm, d), lambda i: (i, 0)),
                  pl.BlockSpec(mod.shape, lambda i: (0, 0)),
                  pl.BlockSpec((1, d), lambda i: (0, 0)),
                  pl.BlockSpec((ne, d), lambda i: (0, 0))],
        out_specs=[pl.BlockSpec((tm, d), lambda i: (i, 0)), pl.BlockSpec((ne, tm), lambda i: (0, i))],
        out_shape=[jax.ShapeDtypeStruct((m, d), F32), jax.ShapeDtypeStruct((ne, m), F32)],
        compiler_params=_params(1), name="moe_router",
    )(x, mod, g.reshape(1, d), router_t)


def _select_kernel(aff_ref, tri_ref, idx_ref, csum_ref, *, tb, c_len, caps):
    bits = pltpu.bitcast(aff_ref[...], jnp.int32)
    ne = bits.shape[0]
    lane = lax.broadcasted_iota(jnp.int32, bits.shape, 1)
    tri = tri_ref[...]
    for (lo, hi), cap, out_off in zip(((0, c_len), (c_len, tb)), caps, (caps[1], 0)):
        vals = jnp.where((lane >= lo) & (lane < hi), bits, -1)

        def refine(it, thr, vals=vals, cap=cap):
            cand = thr | lax.shift_left(jnp.int32(1), 30 - it)
            cnt = jnp.sum(jnp.where(vals >= cand, 1.0, 0.0), axis=-1, keepdims=True)
            return jnp.where(cnt >= cap, cand, thr)

        thr = lax.fori_loop(0, 31, refine, jnp.zeros((ne, 1), jnp.int32))
        gt = vals > thr
        need = cap - jnp.sum(jnp.where(gt, 1.0, 0.0), axis=-1, keepdims=True)
        eq = jnp.where(vals == thr, 1.0, 0.0)
        run_eq = jnp.zeros((ne, 1), F32)
        run_sel = jnp.zeros((ne, 1), F32)
        for j in range(lo // 128, hi // 128):
            blk = eq[:, j * 128:(j + 1) * 128]
            pre = jnp.dot(blk.astype(BF16), tri, preferred_element_type=F32) + run_eq
            run_eq = run_eq + jnp.sum(blk, axis=-1, keepdims=True)
            sel = jnp.where(gt[:, j * 128:(j + 1) * 128] | ((blk > 0) & (pre <= need)), 1.0, 0.0)
            csum_ref[:, j * 128:(j + 1) * 128] = jnp.dot(sel.astype(BF16), tri, preferred_element_type=F32) + run_sel
            run_sel = run_sel + jnp.sum(sel, axis=-1, keepdims=True)

        n_seg = hi - lo
        ct = min(n_seg, 1024)
        slot = lax.broadcasted_iota(jnp.int32, (cap, 1), 0).astype(F32)
        ones = jnp.ones((ct, 128), BF16)

        def per_expert(e, carry, lo=lo, n_seg=n_seg, ct=ct, cap=cap, out_off=out_off, slot=slot, ones=ones):
            cnt = jnp.zeros((cap, 128), F32)
            for c0 in range(0, n_seg, ct):
                row = csum_ref[pl.ds(e, 1), lo + c0:lo + c0 + ct]
                cnt = cnt + jnp.dot(jnp.where(row <= slot, 1.0, 0.0).astype(BF16), ones, preferred_element_type=F32)
            idx_ref[e, 0, out_off:out_off + cap, :] = cnt.astype(jnp.int32)
            return carry

        lax.fori_loop(0, ne, per_expert, 0)


def _select(aff_t, tri, *, n_batch, tb, c_len, caps):
    ne, m = aff_t.shape
    cap_tot = caps[0] + caps[1]
    return pl.pallas_call(
        functools.partial(_select_kernel, tb=tb, c_len=c_len, caps=caps), grid=(n_batch,),
        in_specs=[pl.BlockSpec((ne, tb), lambda b: (0, b)), pl.BlockSpec(tri.shape, lambda b: (0, 0))],
        out_specs=pl.BlockSpec((ne, 1, cap_tot, 128), lambda b: (0, b, 0, 0)),
        out_shape=jax.ShapeDtypeStruct((ne, n_batch, cap_tot, 128), jnp.int32),
        scratch_shapes=[pltpu.VMEM((ne, tb), F32)],
        compiler_params=_params(1), name="moe_select",
    )(aff_t, tri)


MOE_FF_CHUNK = 256
MOE_DMA_UNROLL = 8


def _moe_kernel(idx_ref, gate_ref, mod_ref, wg_ref, wu_ref, wd_ref, h_hbm, x_in, x_hbm, ys, xb, xrow, sem,
                *, rows, n_batch, cap_lat, d, n_ff):
    del x_in
    e, b, f = pl.program_id(0), pl.program_id(1), pl.program_id(2)
    base = (e * n_batch + b) * rows

    def gather_h(s):
        return pltpu.make_async_copy(h_hbm.at[pl.ds(idx_ref[base + s], 1), :], ys.at[pl.ds(s, 1), :], sem.at[0])

    def gather_x(s):
        return pltpu.make_async_copy(x_hbm.at[pl.ds(idx_ref[base + s], 1), :], xrow.at[pl.ds(s, 1), :], sem.at[1])

    def scatter_x(s):
        return pltpu.make_async_copy(xrow.at[pl.ds(s, 1), :], x_hbm.at[pl.ds(idx_ref[base + s], 1), :], sem.at[2])

    def each_row(fn):
        def body(s, c):
            fn(s)
            return c
        lax.fori_loop(0, rows, body, 0, unroll=MOE_DMA_UNROLL)

    @pl.when(f == 0)
    def _():
        each_row(lambda s: (gather_h(s).start(), gather_x(s).start()))
        each_row(lambda s: gather_h(s).wait())
        xb[...] = ys[...].astype(BF16)
        ys[...] = jnp.zeros_like(ys)

    x = xb[...]
    hid = (_silu(jnp.dot(x, wg_ref[0, 0].astype(BF16), preferred_element_type=F32))
           * jnp.dot(x, wu_ref[0, 0].astype(BF16), preferred_element_type=F32))
    ys[...] += jnp.dot(hid.astype(BF16), wd_ref[0, 0].astype(BF16), preferred_element_type=F32)

    @pl.when(f == n_ff - 1)
    def _():
        each_row(lambda s: gather_x(s).wait())
        is_ctx = lax.broadcasted_iota(jnp.int32, (rows, 1), 0) >= cap_lat
        g_mlp = _mod_rows(mod_ref, b, n_batch, 5, d, is_ctx)
        gate = gate_ref[0]
        for j in range(d // 128):
            sl = slice(j * 128, (j + 1) * 128)
            xrow[:, sl] = xrow[:, sl] + g_mlp[:, sl] * (ys[:, sl] * gate)
        each_row(lambda s: scatter_x(s).start())
        each_row(lambda s: scatter_x(s).wait())


def _moe(idx, gate_b, mod, wg, wu, wd, layer, h2, x, *, n_batch, rows, cap_lat):
    m, d = x.shape
    _, ne, _, ff = wg.shape
    fc = MOE_FF_CHUNK
    n_ff = ff // fc
    kern = functools.partial(_moe_kernel, rows=rows, n_batch=n_batch, cap_lat=cap_lat, d=d, n_ff=n_ff)
    grid_spec = pltpu.PrefetchScalarGridSpec(
        num_scalar_prefetch=1, grid=(ne, n_batch, n_ff),
        in_specs=[pl.BlockSpec((1, rows, 128), lambda e, b, f, idx: (e * n_batch + b, 0, 0)),
                  pl.BlockSpec(mod.shape, lambda e, b, f, idx: (0, 0)),
                  pl.BlockSpec((1, 1, d, fc), lambda e, b, f, idx: (layer, e, 0, f)),
                  pl.BlockSpec((1, 1, d, fc), lambda e, b, f, idx: (layer, e, 0, f)),
                  pl.BlockSpec((1, 1, fc, d), lambda e, b, f, idx: (layer, e, f, 0)),
                  pl.BlockSpec(memory_space=pl.ANY),
                  pl.BlockSpec(memory_space=pl.ANY)],
        out_specs=pl.BlockSpec(memory_space=pl.ANY),
        scratch_shapes=[pltpu.VMEM((rows, d), F32), pltpu.VMEM((rows, d), BF16), pltpu.VMEM((rows, d), F32),
                        pltpu.SemaphoreType.DMA((3,))])
    return pl.pallas_call(
        kern, grid_spec=grid_spec, out_shape=jax.ShapeDtypeStruct((m, d), F32),
        input_output_aliases={7: 0},
        compiler_params=pltpu.CompilerParams(dimension_semantics=("arbitrary",) * 3, vmem_limit_bytes=VMEM_LIMIT,
                                             has_side_effects=True),
        name="moe_experts",
    )(idx, gate_b, mod, wg, wu, wd, h2, x)


def _final_kernel(x_ref, g_ref, o_ref):
    o_ref[0] = _rms(x_ref[0]) * g_ref[...]


def _final_norm(x3, g, *, c_len, seq):
    n_batch, _, d = x3.shape
    tr = _pick(math.gcd(c_len, seq), (256, 128, 64, 32, 16, 8))
    off = c_len // tr
    return pl.pallas_call(
        _final_kernel, grid=(n_batch, seq // tr),
        in_specs=[pl.BlockSpec((1, tr, d), lambda b, i: (b, i + off, 0)), pl.BlockSpec((1, d), lambda b, i: (0, 0))],
        out_specs=pl.BlockSpec((1, tr, d), lambda b, i: (b, i, 0)),
        out_shape=jax.ShapeDtypeStruct((n_batch, seq, d), F32),
        compiler_params=_params(2), name="final_norm",
    )(x3, g.reshape(1, d))


def _block_ones(n, blk):
    i = jnp.arange(n)
    return (i[:, None] // blk == i[None, :] // blk).astype(BF16)


def _rope_tables(seq, c_len):
    rows = seq // GRID_W
    row = jnp.repeat(jnp.arange(rows, dtype=F32), GRID_W)
    col = jnp.tile(jnp.arange(GRID_W, dtype=F32), rows)
    quarter = AT_HEAD_DIM // 4
    freqs = ROPE_THETA ** (-jnp.arange(quarter, dtype=F32) / quarter)
    ang_r, ang_c = row[:, None] * freqs, col[:, None] * freqs
    cos = jnp.concatenate([jnp.cos(ang_r)] * 2 + [jnp.cos(ang_c)] * 2, axis=1)
    sin = jnp.concatenate([-jnp.sin(ang_r), jnp.sin(ang_r), -jnp.sin(ang_c), jnp.sin(ang_c)], axis=1)
    cos = jnp.concatenate([jnp.ones((c_len, AT_HEAD_DIM), F32), cos], axis=0)
    sin = jnp.concatenate([jnp.zeros((c_len, AT_HEAD_DIM), F32), sin], axis=0)
    return cos, sin


def kernel(x, c, ctx, c_ctx, mod_w, mod_b, norm1_g, norm2_g, w_in, rwkv_conv, rwkv_w0, rwkv_w2, rwkv_a0, rwkv_a2, rwkv_g2, rwkv_kk, rwkv_ka, rwkv_rk, rwkv_ln_g, rwkv_ln_b, s5_lam_re, s5_lam_im, s5_log_dt, s5_b_re, s5_b_im, s5_c_re, s5_c_im, s5_d, s5_glu, attn_qn, attn_kn, w_branch, w_out, router, exp_gate, exp_up, exp_down, final_g):
    n_batch, seq, d = x.shape
    c_len = ctx.shape[1]
    depth = mod_w.shape[0]
    tb = c_len + seq
    m = n_batch * tb
    tc = math.gcd(c_len, 256)
    ne = router.shape[2]
    cap_ctx, cap_lat = CAPACITY_FACTOR * c_len // ne, CAPACITY_FACTOR * seq // ne
    rows = cap_ctx + cap_lat

    xa = jnp.concatenate([ctx, x], axis=1).reshape(m, d)
    c_all_t = jnp.concatenate([c, c_ctx[None]], axis=0).T
    mods = _modulation(c_all_t, mod_w, mod_b)
    ones_head = _block_ones(RW_WIDTH, RW_HEAD_DIM)
    tri = (jnp.arange(128)[:, None] <= jnp.arange(128)[None, :]).astype(BF16)
    cos_t, sin_t = _rope_tables(seq, c_len)
    s5_perms = _s5_perms(tc)
    row_base = (jnp.arange(n_batch, dtype=jnp.int32) * tb)[None, :, None]
    seg_base = jnp.concatenate([jnp.full((cap_lat,), c_len, jnp.int32), jnp.zeros((cap_ctx,), jnp.int32)])[None, None, :]

    for l in range(depth):
        mod = mods[l]
        p = _inproj(xa, mod, norm1_g[l], w_in, l, tb=tb, c_len=c_len, n_batch=n_batch)

        r, v, kk, w0, kd0, b0, w1, kd1, b1, bonus, g = _rwkv_prep(
            p, rwkv_conv[l], rwkv_w0[l], rwkv_w2[l], rwkv_a0[l], rwkv_a2[l], rwkv_g2[l], rwkv_kk[l],
            rwkv_ka[l], rwkv_rk[l].reshape(-1), ones_head, tb=tb, c_len=c_len)
        o0, o1 = _rwkv_scan(r, v, kk, w0, kd0, b0, w1, kd1, b1, n_batch=n_batch, tb=tb, c_len=c_len)
        y_rw = _rwkv_finish(o0, o1, bonus, g, rwkv_ln_g[l], rwkv_ln_b[l], ones_head, tb=tb)

        a_rows, pws, bres, bims, cres, cims = [], [], [], [], [], []
        for z in range(2):
            pws += _s5_powers(s5_lam_re[l, z], s5_lam_im[l, z], s5_log_dt[l, z], tc // S5_SEGMENTS)
            ab_re, ab_im, bb_re, bb_im = _s5_discretize(s5_lam_re[l, z], s5_lam_im[l, z], s5_log_dt[l, z],
                                                        s5_b_re[l, z], s5_b_im[l, z])
            a_rows += [ab_re.reshape(1, S5_NX), ab_im.reshape(1, S5_NX)]
            bres.append(_s5_pack_in(bb_re))
            bims.append(_s5_pack_in(bb_im))
            cres.append(_s5_pack_out(s5_c_re[l, z]))
            cims.append(_s5_pack_out(s5_c_im[l, z]))
        stack = lambda ts: jnp.stack(ts).astype(BF16)
        ys0, ys1 = _s5_scan(p, jnp.concatenate(a_rows, axis=0), jnp.stack(pws), s5_perms, stack(bres), stack(bims), stack(cres), stack(cims),
                            n_batch=n_batch, tb=tb, c_len=c_len, tc=tc)
        y_s5 = _s5_finish(ys0, ys1, p, s5_d[l], s5_glu[l].astype(BF16), tb=tb)

        q_r, k_r, v_b = _attn_prep(p, cos_t, sin_t, attn_qn[l], attn_kn[l], tb=tb)
        y_at = _flash(q_r, k_r, v_b, n_batch=n_batch, tb=tb, c_len=c_len)

        mm = _merge(y_rw, y_s5, y_at, p, w_branch[l].astype(BF16), tb=tb)
        x1 = _outproj(mm, w_out[l].astype(BF16), xa, mod, tb=tb, c_len=c_len, n_batch=n_batch)

        h2, aff_t = _router(x1, mod, norm2_g[l], router[l].T, tb=tb, c_len=c_len, n_batch=n_batch)
        pos = _select(aff_t, tri, n_batch=n_batch, tb=tb, c_len=c_len, caps=(cap_ctx, cap_lat))[..., 0]
        idx = pos + seg_base + row_base
        gate = jnp.take_along_axis(aff_t, idx.reshape(ne, -1), axis=1)
        gate_b = jnp.broadcast_to(gate.reshape(ne * n_batch, rows, 1), (ne * n_batch, rows, 128))
        xa = _moe(idx.reshape(-1), gate_b, mod, exp_gate, exp_up, exp_down, l, h2, x1,
                  n_batch=n_batch, rows=rows, cap_lat=cap_lat)

    return _final_norm(xa.reshape(n_batch, tb, d), final_g, c_len=c_len, seq=seq)
```

```python
import functools
import math

import jax
import jax.numpy as jnp
from jax import lax
from jax.experimental import pallas as pl
from jax.experimental.pallas import tpu as pltpu

F32 = jnp.float32
BF16 = jnp.bfloat16

NORM_EPS = 1e-6
N_MOD = 6
GRID_W = 64

RW_HEADS = 8
RW_HEAD_DIM = 64
RW_WIDTH = 512
RW_RANK = 64
RW_GATE_RANK = 128
RW_LN_EPS = 64e-5

S5_WIDTH = 512
S5_GROUP = 16
S5_GROUPS = 32
S5_STATE = 64
S5_NX = S5_GROUPS * S5_STATE
S5_PACKS = 4

AT_HEADS = 8
AT_KV_HEADS = 2
AT_GROUP = AT_HEADS // AT_KV_HEADS
AT_HEAD_DIM = 128
AT_WIDTH = 1024
AT_KV_WIDTH = 256
ROPE_THETA = 10000.0

N_EXPERTS = 16
CAPACITY_FACTOR = 2

OFF_WD = 3 * RW_WIDTH
OFF_S5 = OFF_WD + 4 * RW_RANK + RW_GATE_RANK
OFF_Q = OFF_S5 + S5_WIDTH
OFF_K = OFF_Q + AT_WIDTH
OFF_GATE = OFF_K + 2 * AT_KV_WIDTH

P_RW = 0
P_S5 = 2048
P_KV = 2560
P_Q = 3072
P_GATE = 4096

VMEM_LIMIT = 56 * 1024 * 1024


def _params(n_axes, vmem=VMEM_LIMIT):
    return pltpu.CompilerParams(dimension_semantics=("arbitrary",) * n_axes, vmem_limit_bytes=vmem)


def _pick(n, cands):
    for t in cands:
        if n % t == 0:
            return t
    raise ValueError(f"no tile for {n}")


def _split3(x):
    hi = x.astype(BF16)
    r1 = x - hi.astype(F32)
    mid = r1.astype(BF16)
    lo = (r1 - mid.astype(F32)).astype(BF16)
    return hi, mid, lo


def _dot_exact_rhs(x, w_bf16, pieces=3):
    parts = _split3(x)[:pieces]
    acc = jnp.dot(parts[0], w_bf16, preferred_element_type=F32)
    for p in parts[1:]:
        acc = acc + jnp.dot(p, w_bf16, preferred_element_type=F32)
    return acc


def _dot_f32(a, b, dims=(((1,), (0,)), ((), ()))):
    a1, a2, a3 = _split3(a)
    b1, b2, b3 = _split3(b)
    d = functools.partial(lax.dot_general, dimension_numbers=dims, preferred_element_type=F32)
    return (d(a1, b1) + (d(a1, b2) + d(a2, b1))) + ((d(a1, b3) + d(a3, b1)) + d(a2, b2))


def _sigmoid(x):
    return 1.0 / (1.0 + jnp.exp(-x))


def _silu(x):
    return x * _sigmoid(x)


def _row_is_ctx(i, tiles_per_batch, tm, c_len):
    pos = (i % tiles_per_batch) * tm + lax.broadcasted_iota(jnp.int32, (tm, 1), 0)
    return pos < c_len


def _mod_rows(mod_ref, b, n_batch, which, d, is_ctx):
    lat = mod_ref[pl.ds(b, 1), which * d:(which + 1) * d]
    ctx = mod_ref[n_batch:n_batch + 1, which * d:(which + 1) * d]
    return jnp.where(is_ctx, ctx, lat)


def _rms(x):
    return x * lax.rsqrt(jnp.mean(x * x, axis=-1, keepdims=True) + NORM_EPS)


def _mod_kernel(ct_ref, w_ref, b_ref, o_ref, *, n_rows):
    w = w_ref[0]
    for m in range(n_rows):
        col = ct_ref[:, m:m + 1]
        o_ref[0, m:m + 1, :] = jnp.sum(w * _silu(col), axis=0, keepdims=True) + b_ref[0]


def _modulation(c_all_t, mod_w, mod_b):
    depth, d, n6 = mod_w.shape
    n_rows = c_all_t.shape[1]
    tn = 512
    return pl.pallas_call(
        functools.partial(_mod_kernel, n_rows=n_rows),
        grid=(depth, n6 // tn),
        in_specs=[pl.BlockSpec((d, n_rows), lambda l, n: (0, 0)),
                  pl.BlockSpec((1, d, tn), lambda l, n: (l, 0, n)),
                  pl.BlockSpec((1, 1, tn), lambda l, n: (l, 0, n))],
        out_specs=pl.BlockSpec((1, n_rows, tn), lambda l, n: (l, 0, n)),
        out_shape=jax.ShapeDtypeStruct((depth, n_rows, n6), F32),
        compiler_params=_params(2), name="modulation",
    )(c_all_t, mod_w, mod_b.reshape(depth, 1, n6))


def _inproj_kernel(x_ref, mod_ref, g_ref, w_ref, o_ref, h_ref, *, tpb, tm, c_len, n_batch, d):
    i = pl.program_id(0)

    @pl.when(pl.program_id(1) == 0)
    def _():
        b = i // tpb
        n_chunks = next(n for n in (4, 3, 2, 1) if tm % (16 * n) == 0)
        rc = tm // n_chunks
        for j in range(n_chunks):
            rows = slice(j * rc, (j + 1) * rc)
            is_ctx = _row_is_ctx(i, tpb, tm, c_len)[rows]
            shift = _mod_rows(mod_ref, b, n_batch, 0, d, is_ctx)
            scale = _mod_rows(mod_ref, b, n_batch, 1, d, is_ctx)
            h = _rms(x_ref[rows, :]) * g_ref[...]
            h_ref[rows, :] = (h * (1 + scale) + shift).astype(BF16)

    o_ref[...] = jnp.dot(h_ref[...], w_ref[0].astype(BF16), preferred_element_type=F32)


INPROJ_TN = 512


def _w_in_offset(n):
    u = 128
    t = INPROJ_TN // u
    units = jnp.where(n < P_S5 // INPROJ_TN, n * t,
            jnp.where(n < P_KV // INPROJ_TN, OFF_S5 // u + (n - P_S5 // INPROJ_TN) * t,
            jnp.where(n < P_Q // INPROJ_TN, OFF_K // u + (n - P_KV // INPROJ_TN) * t,
            jnp.where(n < P_GATE // INPROJ_TN, OFF_Q // u + (n - P_Q // INPROJ_TN) * t,
                      OFF_GATE // u + (n - P_GATE // INPROJ_TN) * t))))
    return units * u


def _inproj(x, mod, g, w_in, layer, *, tb, c_len, n_batch):
    m, d = x.shape
    n_out = P_GATE + w_in.shape[2] - OFF_GATE
    tm = _pick(tb, (1056, 1024, 768, 640, 512, 256, 128))
    tn = INPROJ_TN
    kern = functools.partial(_inproj_kernel, tpb=tb // tm, tm=tm, c_len=c_len, n_batch=n_batch, d=d)
    return pl.pallas_call(
        kern, grid=(m // tm, n_out // tn),
        in_specs=[pl.BlockSpec((tm, d), lambda i, n: (i, 0)),
                  pl.BlockSpec(mod.shape, lambda i, n: (0, 0)),
                  pl.BlockSpec((1, d), lambda i, n: (0, 0)),
                  pl.BlockSpec((pl.Element(1), pl.Element(d), pl.Element(tn)), lambda i, n: (layer, 0, _w_in_offset(n)))],
        out_specs=pl.BlockSpec((tm, tn), lambda i, n: (i, n)),
        out_shape=jax.ShapeDtypeStruct((m, n_out), F32),
        scratch_shapes=[pltpu.VMEM((tm, d), BF16)],
        compiler_params=_params(2), name="inproj",
    )(x, mod, g.reshape(1, d), w_in)


def _softplus(z):
    return jnp.maximum(z, 0.0) + jnp.log(1.0 + jnp.exp(-jnp.abs(z)))


def _rwkv_prep_kernel(p_ref, prev_ref, next_ref, conv_ref, w0_ref, w2_ref, a0_ref, a2_ref, g2_ref,
                      kk_ref, ka_ref, rk_ref, ones_ref,
                      r_o, v_o, kk_o, w0_o, kd0_o, b0_o, w1_o, kd1_o, b1_o, bonus_o, g_o,
                      *, tr, tpb, c_len, tb):
    i = pl.program_id(0)
    c = RW_WIDTH
    pos = (i % tpb) * tr + lax.broadcasted_iota(jnp.int32, (tr, 1), 0)
    row = lax.broadcasted_iota(jnp.int32, (tr, 1), 0)
    first = (pos == 0) | (pos == c_len)
    last = (pos == c_len - 1) | (pos == tb - 1)

    x = p_ref[:, 0:3 * c]
    x_prev = jnp.where(row == 0, prev_ref[7:8, 0:3 * c], pltpu.roll(x, 1, 0))
    x_next = jnp.where(row == tr - 1, next_ref[0:1, 0:3 * c], pltpu.roll(x, tr - 1, 0))
    x_prev = jnp.where(first, 0.0, x_prev)
    x_next = jnp.where(last, 0.0, x_next)
    rkv = x_prev * conv_ref[0:1, :] + x * conv_ref[1:2, :] + x_next * conv_ref[2:3, :]
    r, k, v = rkv[:, 0:c], rkv[:, c:2 * c], rkv[:, 2 * c:3 * c]
    ones = ones_ref[...]

    kk = k * kk_ref[...]
    kk = kk * lax.rsqrt(_dot_exact_rhs(kk * kk, ones) + 1e-12)
    r_o[...] = r
    v_o[...] = v
    kk_o[...] = kk

    kd_sum = jnp.zeros_like(k)
    for z, (w_o, kd_o, b_o) in enumerate(((w0_o, kd0_o, b0_o), (w1_o, kd1_o, b1_o))):
        wd = jnp.tanh(p_ref[:, OFF_WD + z * RW_RANK:OFF_WD + (z + 1) * RW_RANK])
        ad = p_ref[:, OFF_WD + (2 + z) * RW_RANK:OFF_WD + (3 + z) * RW_RANK]
        w_lin = w0_ref[z:z + 1, :] + _dot_f32(wd, w2_ref[z])
        w_log = -_softplus(-w_lin) - 0.5
        w_o[...] = -jnp.exp(w_log)
        a = _sigmoid(a0_ref[z:z + 1, :] + _dot_f32(ad, a2_ref[z]))
        kd = k * (1 + (a - 1) * ka_ref[...])
        kd_o[...] = kd
        b_o[...] = kk * a
        kd_sum = kd_sum + kd

    bonus_o[...] = _dot_exact_rhs(r * kd_sum * rk_ref[...], ones) * v
    gd = _sigmoid(p_ref[:, OFF_WD + 4 * RW_RANK:OFF_WD + 4 * RW_RANK + RW_GATE_RANK])
    g_o[...] = _dot_f32(gd, g2_ref[...])


def _rwkv_prep(p, conv_w, w0, w2, a0, a2, g2, k_k, k_a, r_k, ones_bd, *, tb, c_len):
    m = p.shape[0]
    c = RW_WIDTH
    tr = _pick(tb, (528, 512, 320, 256, 128))
    nblk8 = m // 8
    kern = functools.partial(_rwkv_prep_kernel, tr=tr, tpb=tb // tr, c_len=c_len, tb=tb)
    full = lambda a: pl.BlockSpec(a.shape, lambda i: (0,) * a.ndim)
    args = (conv_w, w0, w2, a0, a2, g2, k_k.reshape(1, c), k_a.reshape(1, c), r_k.reshape(1, c), ones_bd)
    out = jax.ShapeDtypeStruct((m, c), F32)
    return pl.pallas_call(
        kern, grid=(m // tr,),
        in_specs=[pl.BlockSpec((tr, 2048), lambda i: (i, 0)),
                  pl.BlockSpec((8, 2048), lambda i: (jnp.maximum(i * (tr // 8) - 1, 0), 0)),
                  pl.BlockSpec((8, 2048), lambda i: (jnp.minimum((i + 1) * (tr // 8), nblk8 - 1), 0))]
                 + [full(a) for a in args],
        out_specs=[pl.BlockSpec((tr, c), lambda i: (i, 0))] * 11,
        out_shape=[out] * 11,
        compiler_params=_params(1), name="rwkv_prep",
    )(p, p, p, *args)


def _rev_chunk(c, n_ctx_chunks, n_chunks):
    return jnp.where(c < n_ctx_chunks, n_ctx_chunks - 1 - c, n_chunks - 1 - (c - n_ctx_chunks))


RW_CHUNK = 64

_NN = (((1,), (0,)), ((), ()))
_NT = (((1,), (1,)), ((), ()))
_TN = (((0,), (0,)), ((), ()))


def _mm(a, b, dims=_NN, pa=1, pb=1):
    xs, ys = _split3(a)[:pa], _split3(b)[:pb]
    acc = None
    for i, x in enumerate(xs):
        for j, y in enumerate(ys):
            if i + j < max(pa, pb):
                d = lax.dot_general(x, y, dims, preferred_element_type=F32)
                acc = d if acc is None else acc + d
    return acc


_RW_PIECES = {"gram": (1, 1), "inverse": (2, 2), "values": (1, 1), "solve": (1, 1), "readout": (1, 1),
              "carry": (1, 1), "state": (2, 2)}


RW_NEWTON_STEPS = 1


def _rwkv_scan_kernel(r0, v0, k0, w0, d0, b0, r1, v1, k1, w1, d1, b1, o0, o1, h_ref):
    t = RW_CHUNK
    t2 = 2 * t

    @pl.when(pl.program_id(1) == 0)
    def _():
        h_ref[...] = jnp.zeros_like(h_ref)

    r2 = lax.broadcasted_iota(jnp.int32, (t2, t2), 0)
    c2 = lax.broadcasted_iota(jnp.int32, (t2, t2), 1)
    same = (r2 // t) == (c2 // t)
    eye = r2 == c2
    row = lax.broadcasted_iota(jnp.int32, (t, t), 0)
    col = lax.broadcasted_iota(jnp.int32, (t, t), 1)
    head0 = lax.broadcasted_iota(jnp.int32, (t, t2), 1) < RW_HEAD_DIM

    def stack(x):
        return jnp.concatenate([jnp.where(head0, x, 0.0), jnp.where(head0, 0.0, x)], axis=0)

    combos = []
    for z, (r, v, kk, w, kd, bb, o) in enumerate(((r0, v0, k0, w0, d0, b0, o0), (r1, v1, k1, w1, d1, b1, o1))):
        before = (c2 % t < r2 % t) if z == 0 else (c2 % t > r2 % t)
        strict = same & before
        incl = same & (before | eye)
        cum = jnp.where((col <= row) if z == 0 else (col >= row), 1.0, 0.0)
        lw_all = w[...]
        g_all = _mm(cum, lw_all, pb=3)
        gtot_all = g_all[t - 1:t, :] if z == 0 else g_all[0:1, :]
        for hp in range(RW_WIDTH // t2):
            sl = slice(hp * t2, (hp + 1) * t2)
            g = g_all[:, sl]
            e_inv = jnp.exp(-g)
            combos.append(dict(
                z=z, hp=hp, sl=sl, o=o, strict=strict, incl=incl, gtot=jnp.exp(gtot_all[:, sl]),
                a_st=stack(kk[:, sl] * jnp.exp(g - lw_all[:, sl])), r_st=stack(r[:, sl] * jnp.exp(g)),
                bt=bb[:, sl] * e_inv, kt=kd[:, sl] * e_inv, v_st=stack(v[:, sl])))

    for c in combos:
        mm = _mm(jnp.concatenate([c["a_st"], c["r_st"]], axis=0),
                 jnp.concatenate([c["bt"], c["bt"], c["kt"], c["kt"]], axis=0), _NT, *_RW_PIECES["gram"])
        c["n_pow"] = jnp.where(c["strict"], -mm[:t2, :t2], 0.0)
        c["l_ak"] = jnp.where(c["strict"], mm[:t2, t2:], 0.0)
        c["m_rb"] = jnp.where(c["incl"], mm[t2:, :t2], 0.0)
        c["m_rk"] = jnp.where(c["incl"], mm[t2:, t2:], 0.0)
        c["inv"] = jnp.where(eye, 1.0, 0.0) + c["n_pow"]
    for c in combos:
        c["y"] = _mm(jnp.concatenate([c["l_ak"], c["m_rk"]], axis=0), c["v_st"], _NN, *_RW_PIECES["values"])
    for c in combos:
        c["a_mat"] = jnp.where(eye, 1.0, 0.0) - c["n_pow"]
    for _ in range(5):
        for c in combos:
            c["n_pow"] = _mm(c["n_pow"], c["n_pow"])
        for c in combos:
            c["inv"] = c["inv"] + _mm(c["inv"], c["n_pow"])
    for _ in range(RW_NEWTON_STEPS):
        for c in combos:
            c["res"] = jnp.where(eye, 1.0, 0.0) - _mm(c["a_mat"], c["inv"], _NN, *_RW_PIECES["inverse"])
        for c in combos:
            c["inv"] = c["inv"] + _mm(c["inv"], c["res"])
    for c in combos:
        c["zz"] = _mm(c["inv"], jnp.concatenate([c["a_st"], c["y"][:t2]], axis=1), _NN, *_RW_PIECES["solve"])
    for c in combos:
        ro = _mm(c["m_rb"], c["zz"], _NN, *_RW_PIECES["readout"])
        c["r_hat"] = c["r_st"] - ro[:, :t2]
        c["o_hat"] = c["y"][t2:] - ro[:, t2:]
    for c in combos:
        btw = _mm(stack(c["bt"] * c["gtot"]), c["zz"], _TN, *_RW_PIECES["carry"])
        c["p"] = jnp.where(eye, c["gtot"], 0.0) - btw[:, :t2]
        c["q"] = _mm(stack(c["kt"] * c["gtot"]), c["v_st"], _TN, *_RW_PIECES["carry"]) - btw[:, t2:]
    for c in combos:
        h = h_ref[c["z"], c["hp"]]
        o_st = _mm(c["r_hat"], h, _NN, *_RW_PIECES["state"]) + c["o_hat"]
        c["o"][:, c["sl"]] = o_st[:t] + o_st[t:]
        h_ref[c["z"], c["hp"]] = _mm(c["p"], h, _NN, *_RW_PIECES["state"]) + c["q"]


def _rwkv_scan(r, v, kk, w0, kd0, b0, w1, kd1, b1, *, n_batch, tb, c_len):
    m, c = r.shape
    tc = RW_CHUNK
    nc, ncc = tb // tc, c_len // tc
    fwd = pl.BlockSpec((tc, c), lambda b, j: (b * nc + j, 0))
    rev = pl.BlockSpec((tc, c), lambda b, j: (b * nc + _rev_chunk(j, ncc, nc), 0))
    out = jax.ShapeDtypeStruct((m, c), F32)
    return pl.pallas_call(
        _rwkv_scan_kernel, grid=(n_batch, nc),
        in_specs=[fwd] * 6 + [rev] * 6,
        out_specs=[fwd, rev], out_shape=[out, out],
        scratch_shapes=[pltpu.VMEM((2, c // (2 * RW_HEAD_DIM), 2 * RW_HEAD_DIM, 2 * RW_HEAD_DIM), F32)],
        compiler_params=_params(2), name="rwkv_scan",
    )(r, v, kk, w0, kd0, b0, r, v, kk, w1, kd1, b1)


def _rwkv_finish_kernel(o0, o1, bonus, g, lng, lnb, ones_ref, y_ref):
    o = o0[...] + o1[...]
    ones = ones_ref[...]
    mu = _dot_exact_rhs(o, ones) * (1.0 / RW_HEAD_DIM)
    dev = o - mu
    var = _dot_exact_rhs(dev * dev, ones) * (1.0 / RW_HEAD_DIM)
    o_n = dev * lax.rsqrt(var + RW_LN_EPS)
    y_ref[...] = ((o_n * lng[...] + lnb[...] + bonus[...]) * g[...]).astype(BF16)


def _rwkv_finish(o0, o1, bonus, g, ln_g, ln_b, ones_bd, *, tb):
    m, c = o0.shape
    tr = _pick(tb, (1056, 1024, 640, 512, 256, 128))
    blk = pl.BlockSpec((tr, c), lambda i: (i, 0))
    vec = pl.BlockSpec((1, c), lambda i: (0, 0))
    return pl.pallas_call(
        _rwkv_finish_kernel, grid=(m // tr,),
        in_specs=[blk] * 4 + [vec, vec, pl.BlockSpec(ones_bd.shape, lambda i: (0, 0))],
        out_specs=blk, out_shape=jax.ShapeDtypeStruct((m, c), BF16),
        compiler_params=_params(1), name="rwkv_finish",
    )(o0, o1, bonus, g, ln_g.reshape(1, c), ln_b.reshape(1, c), ones_bd)


def _s5_discretize(lam_re, lam_im, log_dt, b_re, b_im):
    dt = jnp.exp(log_dt)[:, None]
    mag = jnp.exp(lam_re * dt)
    ab_re, ab_im = mag * jnp.cos(lam_im * dt), mag * jnp.sin(lam_im * dt)
    den = lam_re * lam_re + lam_im * lam_im
    num_re = ab_re - 1.0
    co_re = (num_re * lam_re + ab_im * lam_im) / den
    co_im = (ab_im * lam_re - num_re * lam_im) / den
    bb_re = co_re[..., None] * b_re - co_im[..., None] * b_im
    bb_im = co_re[..., None] * b_im + co_im[..., None] * b_re
    return ab_re, ab_im, bb_re, bb_im


def _s5_powers(lam_re, lam_im, log_dt, n):
    dt = jnp.exp(log_dt)[None, :, None]
    k = jnp.arange(1, n + 1, dtype=F32)[:, None, None]
    mag = jnp.exp(k * lam_re[None] * dt)
    ang = k * lam_im[None] * dt
    return [(mag * jnp.cos(ang)).reshape(n, S5_NX), (mag * jnp.sin(ang)).reshape(n, S5_NX)]


def _s5_pack_in(bb):
    gpp = S5_GROUPS // S5_PACKS
    bb = bb.reshape(S5_PACKS, gpp, S5_STATE, S5_GROUP)
    eye = jnp.eye(gpp, dtype=bb.dtype)
    t = jnp.einsum('qgnc,gh->qgchn', bb, eye)
    return t.reshape(S5_PACKS, gpp * S5_GROUP, gpp * S5_STATE)


def _s5_pack_out(cc):
    gpp = S5_GROUPS // S5_PACKS
    cc = cc.reshape(S5_PACKS, gpp, S5_GROUP, S5_STATE)
    eye = jnp.eye(gpp, dtype=cc.dtype)
    t = jnp.einsum('qgcn,gh->qgnhc', cc, eye)
    return t.reshape(S5_PACKS, gpp * S5_STATE, gpp * S5_GROUP)


S5_SEGMENTS = 8


def _cmul(ar, ai, br, bi):
    return ar * br - ai * bi, ar * bi + ai * br


def _s5_scan_kernel(u0, u1, a_ref, pw_ref, perm_ref, bre, bim, cre, cim, y0, y1, xr_ref, xi_ref, st_ref, *, tc):
    @pl.when(pl.program_id(1) == 0)
    def _():
        st_ref[...] = jnp.zeros_like(st_ref)

    cw = S5_WIDTH // S5_PACKS
    nw = S5_NX // S5_PACKS
    n_seg = S5_SEGMENTS
    seg = tc // n_seg
    sub = lax.broadcasted_iota(jnp.int32, (n_seg, S5_NX), 0)
    for z, (u, y) in enumerate(((u0, y0), (u1, y1))):
        perm = perm_ref[z]
        ub = jnp.dot(perm, u[...].astype(BF16), preferred_element_type=F32).astype(BF16)
        for q in range(S5_PACKS):
            uq = ub[:, q * cw:(q + 1) * cw]
            xr_ref[:, q * nw:(q + 1) * nw] = jnp.dot(uq, bre[z, q], preferred_element_type=F32)
            xi_ref[:, q * nw:(q + 1) * nw] = jnp.dot(uq, bim[z, q], preferred_element_type=F32)
        ar = a_ref[2 * z:2 * z + 1, :]
        ai = a_ref[2 * z + 1:2 * z + 2, :]

        sr = jnp.zeros((n_seg, S5_NX), F32)
        si = jnp.zeros((n_seg, S5_NX), F32)
        for i in range(seg):
            rows = slice(i * n_seg, (i + 1) * n_seg)
            pr, pi = _cmul(ar, ai, sr, si)
            sr, si = pr + xr_ref[rows, :], pi + xi_ref[rows, :]
            xr_ref[rows, :] = sr
            xi_ref[rows, :] = si

        a_seg_r = pw_ref[2 * z, seg - 1:seg, :]
        a_seg_i = pw_ref[2 * z + 1, seg - 1:seg, :]
        cr, ci = st_ref[2 * z:2 * z + 1, :], st_ref[2 * z + 1:2 * z + 2, :]
        cin_r = jnp.zeros((n_seg, S5_NX), F32)
        cin_i = jnp.zeros((n_seg, S5_NX), F32)
        for k in range(n_seg):
            sg = k if z == 0 else n_seg - 1 - k
            cin_r = jnp.where(sub == sg, cr, cin_r)
            cin_i = jnp.where(sub == sg, ci, cin_i)
            pr, pi = _cmul(a_seg_r, a_seg_i, cr, ci)
            cr, ci = pr + sr[sg:sg + 1, :], pi + si[sg:sg + 1, :]
        st_ref[2 * z:2 * z + 1, :] = cr
        st_ref[2 * z + 1:2 * z + 2, :] = ci

        for i in range(seg):
            rows = slice(i * n_seg, (i + 1) * n_seg)
            pr, pi = _cmul(pw_ref[2 * z, i:i + 1, :], pw_ref[2 * z + 1, i:i + 1, :], cin_r, cin_i)
            xr_ref[rows, :] = xr_ref[rows, :] + pr
            xi_ref[rows, :] = xi_ref[rows, :] + pi

        xr = xr_ref[...].astype(BF16)
        xi = xi_ref[...].astype(BF16)
        yp = jnp.concatenate(
            [jnp.dot(xr[:, q * nw:(q + 1) * nw], cre[z, q], preferred_element_type=F32)
             - jnp.dot(xi[:, q * nw:(q + 1) * nw], cim[z, q], preferred_element_type=F32) for q in range(S5_PACKS)], axis=1)
        y[...] = _mm(perm.astype(F32), yp, _TN, 1, 3)


def _s5_perms(tc):
    n_seg = S5_SEGMENTS
    seg = tc // n_seg
    i = jnp.arange(tc) // n_seg
    sg = jnp.arange(tc) % n_seg
    cols = jnp.stack([sg * seg + i, sg * seg + (seg - 1 - i)])
    return (cols[:, :, None] == jnp.arange(tc)[None, None, :]).astype(BF16)


def _s5_scan(p, a_all, pw, perms, bre, bim, cre, cim, *, n_batch, tb, c_len, tc):
    m = p.shape[0]
    nc, ncc = tb // tc, c_len // tc
    ucol = P_S5 // S5_WIDTH
    full = lambda a: pl.BlockSpec(a.shape, lambda b, j: (0,) * a.ndim)
    out = jax.ShapeDtypeStruct((m, S5_WIDTH), F32)
    return pl.pallas_call(
        functools.partial(_s5_scan_kernel, tc=tc), grid=(n_batch, nc),
        in_specs=[pl.BlockSpec((tc, S5_WIDTH), lambda b, j: (b * nc + j, ucol)),
                  pl.BlockSpec((tc, S5_WIDTH), lambda b, j: (b * nc + _rev_chunk(j, ncc, nc), ucol)),
                  full(a_all), full(pw), full(perms), full(bre), full(bim), full(cre), full(cim)],
        out_specs=[pl.BlockSpec((tc, S5_WIDTH), lambda b, j: (b * nc + j, 0)),
                   pl.BlockSpec((tc, S5_WIDTH), lambda b, j: (b * nc + _rev_chunk(j, ncc, nc), 0))],
        out_shape=[out, out],
        scratch_shapes=[pltpu.VMEM((tc, S5_NX), F32), pltpu.VMEM((tc, S5_NX), F32), pltpu.VMEM((4, S5_NX), F32)],
        compiler_params=_params(2), name="s5_scan",
    )(p, p, a_all, pw, perms, bre, bim, cre, cim)


def _s5_finish_kernel(y0, y1, u, d_ref, w_ref, o_ref):
    y = y0[...] + y1[...] + d_ref[...] * u[...]
    ge = 0.5 * y * (1.0 + jnp.tanh(math.sqrt(2.0 / math.pi) * (y + 0.044715 * (y * y * y))))
    gate = _sigmoid(jnp.dot(ge.astype(BF16), w_ref[...], preferred_element_type=F32))
    o_ref[...] = (ge * gate).astype(BF16)


def _s5_finish(y0, y1, p, d_skip, w_glu, *, tb):
    m, c = y0.shape
    tr = _pick(tb, (1056, 1024, 640, 512, 256, 128))
    blk = pl.BlockSpec((tr, c), lambda i: (i, 0))
    return pl.pallas_call(
        _s5_finish_kernel, grid=(m // tr,),
        in_specs=[blk, blk, pl.BlockSpec((tr, c), lambda i: (i, P_S5 // c)),
                  pl.BlockSpec((1, c), lambda i: (0, 0)), pl.BlockSpec((c, c), lambda i: (0, 0))],
        out_specs=blk, out_shape=jax.ShapeDtypeStruct((m, c), BF16),
        compiler_params=_params(1), name="s5_finish",
    )(y0, y1, p, d_skip.reshape(1, c), w_glu)


def _rope(x, cos, sin_signed):
    lane = lax.broadcasted_iota(jnp.int32, x.shape, 1)
    swapped = jnp.where(lane % 64 < 32, pltpu.roll(x, 96, 1), pltpu.roll(x, 32, 1))
    return x * cos + swapped * sin_signed


def _attn_prep_kernel(kv_ref, q_ref, cos_ref, sin_ref, qn_ref, kn_ref, qo, ko, vo):
    cos, sin = cos_ref[...], sin_ref[...]
    hd = AT_HEAD_DIM
    for h in range(AT_HEADS):
        q = _rms(q_ref[:, h * hd:(h + 1) * hd]) * qn_ref[...]
        qo[:, h * hd:(h + 1) * hd] = (_rope(q, cos, sin) * (hd ** -0.5 * math.log2(math.e))).astype(BF16)
    for h in range(AT_KV_HEADS):
        k = _rms(kv_ref[:, h * hd:(h + 1) * hd]) * kn_ref[...]
        ko[:, h * hd:(h + 1) * hd] = _rope(k, cos, sin).astype(BF16)
    vo[...] = kv_ref[:, AT_KV_WIDTH:2 * AT_KV_WIDTH].astype(BF16)


def _attn_prep(p, cos_t, sin_t, qn, kn, *, tb):
    m = p.shape[0]
    tr = _pick(tb, (528, 512, 320, 256, 128))
    tpb = tb // tr
    hd = AT_HEAD_DIM
    return pl.pallas_call(
        _attn_prep_kernel, grid=(m // tr,),
        in_specs=[pl.BlockSpec((tr, 2 * AT_KV_WIDTH), lambda i: (i, P_KV // (2 * AT_KV_WIDTH))),
                  pl.BlockSpec((tr, AT_WIDTH), lambda i: (i, P_Q // AT_WIDTH)),
                  pl.BlockSpec((tr, hd), lambda i: (i % tpb, 0)),
                  pl.BlockSpec((tr, hd), lambda i: (i % tpb, 0)),
                  pl.BlockSpec((1, hd), lambda i: (0, 0)), pl.BlockSpec((1, hd), lambda i: (0, 0))],
        out_specs=[pl.BlockSpec((tr, AT_WIDTH), lambda i: (i, 0)),
                   pl.BlockSpec((tr, AT_KV_WIDTH), lambda i: (i, 0)),
                   pl.BlockSpec((tr, AT_KV_WIDTH), lambda i: (i, 0))],
        out_shape=[jax.ShapeDtypeStruct((m, AT_WIDTH), BF16), jax.ShapeDtypeStruct((m, AT_KV_WIDTH), BF16),
                   jax.ShapeDtypeStruct((m, AT_KV_WIDTH), BF16)],
        compiler_params=_params(1), name="attn_prep",
    )(p, p, cos_t, sin_t, qn.reshape(1, hd), kn.reshape(1, hd))


FLASH_TQ = (1056, 1024, 640, 512, 256, 128)
FLASH_TK = (768, 640, 512, 384, 256, 128)


def _flash_kernel(q_ref, k_ref, v_ref, o_ref, m_ref, acc_ref, s_ref, p_ref, *, tq, tk, c_len, nk):
    qi, ki = pl.program_id(2), pl.program_id(3)
    hd = AT_HEAD_DIM

    @pl.when(ki == 0)
    def _():
        m_ref[...] = jnp.full_like(m_ref, -1e30)
        acc_ref[...] = jnp.zeros_like(acc_ref)

    def body(masked):
        k = k_ref[...]
        v_ext = jnp.concatenate([v_ref[...], jnp.ones((tk, hd), BF16)], axis=1)
        if masked:
            row = qi * tq + lax.broadcasted_iota(jnp.int32, (tq, 1), 0)
            col = ki * tk + lax.broadcasted_iota(jnp.int32, (1, tk), 1)
            hide = (row < c_len) & (col >= c_len)
        for g in range(AT_GROUP):
            b = g % 2
            s = lax.dot_general(q_ref[:, g * hd:(g + 1) * hd], k, _NT, preferred_element_type=F32)
            s_ref[b] = jnp.where(hide, -1e30, s) if masked else s
            m_prev = m_ref[g]
            m_new = jnp.maximum(m_prev, jnp.max(s_ref[b], axis=-1, keepdims=True))
            m_ref[g] = m_new
            p_ref[b] = jnp.exp2(s_ref[b] - m_new).astype(BF16)
            acc_ref[g] = jnp.exp2(m_prev - m_new) * acc_ref[g] + jnp.dot(p_ref[b], v_ext, preferred_element_type=F32)

    has_ctx = qi * tq < c_len
    pl.when(has_ctx)(lambda: body(True))
    pl.when(jnp.logical_not(has_ctx))(lambda: body(False))

    @pl.when(ki == nk - 1)
    def _():
        for g in range(AT_GROUP):
            o_ref[:, g * hd:(g + 1) * hd] = (acc_ref[g, :, :hd] / acc_ref[g, :, hd:]).astype(BF16)


def _flash(q, k, v, *, n_batch, tb, c_len):
    m = q.shape[0]
    hd = AT_HEAD_DIM
    gw = AT_GROUP * hd
    tq = _pick(tb, FLASH_TQ)
    tk = _pick(tb, FLASH_TK)
    nq, nk = tb // tq, tb // tk
    kern = functools.partial(_flash_kernel, tq=tq, tk=tk, c_len=c_len, nk=nk)
    return pl.pallas_call(
        kern, grid=(n_batch, AT_KV_HEADS, nq, nk),
        in_specs=[pl.BlockSpec((tq, gw), lambda b, h, i, j: (b * nq + i, h)),
                  pl.BlockSpec((tk, hd), lambda b, h, i, j: (b * nk + j, h)),
                  pl.BlockSpec((tk, hd), lambda b, h, i, j: (b * nk + j, h))],
        out_specs=pl.BlockSpec((tq, gw), lambda b, h, i, j: (b * nq + i, h)),
        out_shape=jax.ShapeDtypeStruct((m, AT_WIDTH), BF16),
        scratch_shapes=[pltpu.VMEM((AT_GROUP, tq, 1), F32), pltpu.VMEM((AT_GROUP, tq, 2 * hd), F32),
                        pltpu.VMEM((2, tq, tk), F32), pltpu.VMEM((2, tq, tk), BF16)],
        compiler_params=_params(4), name="flash_attention",
    )(q, k, v)


def _merge_kernel(yr, ys, ya, g0, g1, g2, wr, ws, wa, o_ref):
    m = _sigmoid(g0[...]) * jnp.dot(yr[...], wr[...], preferred_element_type=F32)
    m = m + _sigmoid(g1[...]) * jnp.dot(ys[...], ws[...], preferred_element_type=F32)
    m = m + _sigmoid(g2[...]) * jnp.dot(ya[...], wa[...], preferred_element_type=F32)
    o_ref[...] = m.astype(BF16)


def _merge(y_rw, y_s5, y_at, p, w_branch, *, tb):
    m = p.shape[0]
    d = w_branch.shape[1]
    tm = _pick(tb, (1056, 1024, 640, 512, 256, 128))
    tn = 512
    gcol = P_GATE // tn
    nd = d // tn
    gate = lambda br: pl.BlockSpec((tm, tn), lambda i, n: (i, gcol + br * nd + n))
    return pl.pallas_call(
        _merge_kernel, grid=(m // tm, nd),
        in_specs=[pl.BlockSpec((tm, RW_WIDTH), lambda i, n: (i, 0)),
                  pl.BlockSpec((tm, S5_WIDTH), lambda i, n: (i, 0)),
                  pl.BlockSpec((tm, AT_WIDTH), lambda i, n: (i, 0)),
                  gate(0), gate(1), gate(2),
                  pl.BlockSpec((RW_WIDTH, tn), lambda i, n: (0, n)),
                  pl.BlockSpec((S5_WIDTH, tn), lambda i, n: (1, n)),
                  pl.BlockSpec((AT_WIDTH, tn), lambda i, n: (1, n))],
        out_specs=pl.BlockSpec((tm, tn), lambda i, n: (i, n)),
        out_shape=jax.ShapeDtypeStruct((m, d), BF16),
        compiler_params=_params(2), name="merge",
    )(y_rw, y_s5, y_at, p, p, p, w_branch, w_branch, w_branch)


def _outproj_kernel(m_ref, w_ref, x_ref, mod_ref, o_ref, *, tpb, tm, c_len, n_batch):
    i = pl.program_id(0)
    is_ctx = _row_is_ctx(i, tpb, tm, c_len)
    gate = jnp.where(is_ctx, mod_ref[n_batch:n_batch + 1, :], mod_ref[pl.ds(i // tpb, 1), :])
    o_ref[...] = x_ref[...] + gate * jnp.dot(m_ref[...], w_ref[...], preferred_element_type=F32)


def _outproj(mm, w_out, x, mod, *, tb, c_len, n_batch):
    m, d = x.shape
    tm = _pick(tb, (1056, 1024, 640, 512, 256, 128))
    tn = 512
    kern = functools.partial(_outproj_kernel, tpb=tb // tm, tm=tm, c_len=c_len, n_batch=n_batch)
    return pl.pallas_call(
        kern, grid=(m // tm, d // tn),
        in_specs=[pl.BlockSpec((tm, d), lambda i, n: (i, 0)),
                  pl.BlockSpec((d, tn), lambda i, n: (0, n)),
                  pl.BlockSpec((tm, tn), lambda i, n: (i, n)),
                  pl.BlockSpec((mod.shape[0], tn), lambda i, n: (0, 2 * d // tn + n))],
        out_specs=pl.BlockSpec((tm, tn), lambda i, n: (i, n)),
        out_shape=jax.ShapeDtypeStruct((m, d), F32),
        compiler_params=_params(2), name="outproj",
    )(mm, w_out, x, mod)


def _router_kernel(x_ref, mod_ref, g_ref, rt_ref, h_ref, aff_ref, *, tpb, tm, c_len, n_batch, d):
    i = pl.program_id(0)
    is_ctx = _row_is_ctx(i, tpb, tm, c_len)
    b = i // tpb
    shift = _mod_rows(mod_ref, b, n_batch, 3, d, is_ctx)
    scale = _mod_rows(mod_ref, b, n_batch, 4, d, is_ctx)
    h = _rms(x_ref[...]) * g_ref[...]
    h = h * (1 + scale) + shift
    h_ref[...] = h
    logits = _dot_f32(rt_ref[...], h, dims=(((1,), (1,)), ((), ())))
    e = jnp.exp(logits - jnp.max(logits, axis=0, keepdims=True))
    aff_ref[...] = e / jnp.sum(e, axis=0, keepdims=True)


def _router(x, mod, g, router_t, *, tb, c_len, n_batch):
    m, d = x.shape
    ne = router_t.shape[0]
    tm = _pick(tb, (768, 640, 512, 384, 256, 128))
    kern = functools.partial(_router_kernel, tpb=tb // tm, tm=tm, c_len=c_len, n_batch=n_batch, d=d)
    return pl.pallas_call(
        kern, grid=(m // tm,),
        in_specs=[pl.BlockSpec((tm, d), lambda i: (i, 0)),
                  pl.BlockSpec(mod.shape, lambda i: (0, 0)),
                  pl.BlockSpec((1, d), lambda i: (0, 0)),
                  pl.BlockSpec((ne, d), lambda i: (0, 0))],
        out_specs=[pl.BlockSpec((tm, d), lambda i: (i, 0)), pl.BlockSpec((ne, tm), lambda i: (0, i))],
        out_shape=[jax.ShapeDtypeStruct((m, d), F32), jax.ShapeDtypeStruct((ne, m), F32)],
        compiler_params=_params(1), name="moe_router",
    )(x, mod, g.reshape(1, d), router_t)


def _select_kernel(aff_ref, tri_ref, idx_ref, csum_ref, *, tb, c_len, caps):
    bits = pltpu.bitcast(aff_ref[...], jnp.int32)
    ne = bits.shape[0]
    lane = lax.broadcasted_iota(jnp.int32, bits.shape, 1)
    tri = tri_ref[...]
    for (lo, hi), cap, out_off in zip(((0, c_len), (c_len, tb)), caps, (caps[1], 0)):
        vals = jnp.where((lane >= lo) & (lane < hi), bits, -1)

        def refine(it, thr, vals=vals, cap=cap):
            cand = thr | lax.shift_left(jnp.int32(1), 30 - it)
            cnt = jnp.sum(jnp.where(vals >= cand, 1.0, 0.0), axis=-1, keepdims=True)
            return jnp.where(cnt >= cap, cand, thr)

        thr = lax.fori_loop(0, 31, refine, jnp.zeros((ne, 1), jnp.int32))
        gt = vals > thr
        need = cap - jnp.sum(jnp.where(gt, 1.0, 0.0), axis=-1, keepdims=True)
        eq = jnp.where(vals == thr, 1.0, 0.0)
        run_eq = jnp.zeros((ne, 1), F32)
        run_sel = jnp.zeros((ne, 1), F32)
        for j in range(lo // 128, hi // 128):
            blk = eq[:, j * 128:(j + 1) * 128]
            pre = jnp.dot(blk.astype(BF16), tri, preferred_element_type=F32) + run_eq
            run_eq = run_eq + jnp.sum(blk, axis=-1, keepdims=True)
            sel = jnp.where(gt[:, j * 128:(j + 1) * 128] | ((blk > 0) & (pre <= need)), 1.0, 0.0)
            csum_ref[:, j * 128:(j + 1) * 128] = jnp.dot(sel.astype(BF16), tri, preferred_element_type=F32) + run_sel
            run_sel = run_sel + jnp.sum(sel, axis=-1, keepdims=True)

        n_seg = hi - lo
        ct = min(n_seg, 1024)
        slot = lax.broadcasted_iota(jnp.int32, (cap, 1), 0).astype(F32)
        ones = jnp.ones((ct, 128), BF16)

        def per_expert(e, carry, lo=lo, n_seg=n_seg, ct=ct, cap=cap, out_off=out_off, slot=slot, ones=ones):
            cnt = jnp.zeros((cap, 128), F32)
            for c0 in range(0, n_seg, ct):
                row = csum_ref[pl.ds(e, 1), lo + c0:lo + c0 + ct]
                cnt = cnt + jnp.dot(jnp.where(row <= slot, 1.0, 0.0).astype(BF16), ones, preferred_element_type=F32)
            idx_ref[e, 0, out_off:out_off + cap, :] = cnt.astype(jnp.int32)
            return carry

        lax.fori_loop(0, ne, per_expert, 0)


def _select(aff_t, tri, *, n_batch, tb, c_len, caps):
    ne, m = aff_t.shape
    cap_tot = caps[0] + caps[1]
    return pl.pallas_call(
        functools.partial(_select_kernel, tb=tb, c_len=c_len, caps=caps), grid=(n_batch,),
        in_specs=[pl.BlockSpec((ne, tb), lambda b: (0, b)), pl.BlockSpec(tri.shape, lambda b: (0, 0))],
        out_specs=pl.BlockSpec((ne, 1, cap_tot, 128), lambda b: (0, b, 0, 0)),
        out_shape=jax.ShapeDtypeStruct((ne, n_batch, cap_tot, 128), jnp.int32),
        scratch_shapes=[pltpu.VMEM((ne, tb), F32)],
        compiler_params=_params(1), name="moe_select",
    )(aff_t, tri)


MOE_FF_CHUNK = 256
MOE_DMA_UNROLL = 8


def _moe_kernel(idx_ref, gate_ref, mod_ref, wg_ref, wu_ref, wd_ref, h_hbm, x_in, x_hbm, ys, xb, xrow, sem,
                *, rows, n_batch, cap_lat, d, n_ff):
    del x_in
    e, b, f = pl.program_id(0), pl.program_id(1), pl.program_id(2)
    base = (e * n_batch + b) * rows

    def gather_h(s):
        return pltpu.make_async_copy(h_hbm.at[pl.ds(idx_ref[base + s], 1), :], ys.at[pl.ds(s, 1), :], sem.at[0])

    def gather_x(s):
        return pltpu.make_async_copy(x_hbm.at[pl.ds(idx_ref[base + s], 1), :], xrow.at[pl.ds(s, 1), :], sem.at[1])

    def scatter_x(s):
        return pltpu.make_async_copy(xrow.at[pl.ds(s, 1), :], x_hbm.at[pl.ds(idx_ref[base + s], 1), :], sem.at[2])

    def each_row(fn):
        def body(s, c):
            fn(s)
            return c
        lax.fori_loop(0, rows, body, 0, unroll=MOE_DMA_UNROLL)

    @pl.when(f == 0)
    def _():
        each_row(lambda s: (gather_h(s).start(), gather_x(s).start()))
        each_row(lambda s: gather_h(s).wait())
        xb[...] = ys[...].astype(BF16)
        ys[...] = jnp.zeros_like(ys)

    x = xb[...]
    hid = (_silu(jnp.dot(x, wg_ref[0, 0].astype(BF16), preferred_element_type=F32))
           * jnp.dot(x, wu_ref[0, 0].astype(BF16), preferred_element_type=F32))
    ys[...] += jnp.dot(hid.astype(BF16), wd_ref[0, 0].astype(BF16), preferred_element_type=F32)

    @pl.when(f == n_ff - 1)
    def _():
        each_row(lambda s: gather_x(s).wait())
        is_ctx = lax.broadcasted_iota(jnp.int32, (rows, 1), 0) >= cap_lat
        g_mlp = _mod_rows(mod_ref, b, n_batch, 5, d, is_ctx)
        gate = gate_ref[0]
        for j in range(d // 128):
            sl = slice(j * 128, (j + 1) * 128)
            xrow[:, sl] = xrow[:, sl] + g_mlp[:, sl] * (ys[:, sl] * gate)
        each_row(lambda s: scatter_x(s).start())
        each_row(lambda s: scatter_x(s).wait())


def _moe(idx, gate_b, mod, wg, wu, wd, layer, h2, x, *, n_batch, rows, cap_lat):
    m, d = x.shape
    _, ne, _, ff = wg.shape
    fc = MOE_FF_CHUNK
    n_ff = ff // fc
    kern = functools.partial(_moe_kernel, rows=rows, n_batch=n_batch, cap_lat=cap_lat, d=d, n_ff=n_ff)
    grid_spec = pltpu.PrefetchScalarGridSpec(
        num_scalar_prefetch=1, grid=(ne, n_batch, n_ff),
        in_specs=[pl.BlockSpec((1, rows, 128), lambda e, b, f, idx: (e * n_batch + b, 0, 0)),
                  pl.BlockSpec(mod.shape, lambda e, b, f, idx: (0, 0)),
                  pl.BlockSpec((1, 1, d, fc), lambda e, b, f, idx: (layer, e, 0, f)),
                  pl.BlockSpec((1, 1, d, fc), lambda e, b, f, idx: (layer, e, 0, f)),
                  pl.BlockSpec((1, 1, fc, d), lambda e, b, f, idx: (layer, e, f, 0)),
                  pl.BlockSpec(memory_space=pl.ANY),
                  pl.BlockSpec(memory_space=pl.ANY)],
        out_specs=pl.BlockSpec(memory_space=pl.ANY),
        scratch_shapes=[pltpu.VMEM((rows, d), F32), pltpu.VMEM((rows, d), BF16), pltpu.VMEM((rows, d), F32),
                        pltpu.SemaphoreType.DMA((3,))])
    return pl.pallas_call(
        kern, grid_spec=grid_spec, out_shape=jax.ShapeDtypeStruct((m, d), F32),
        input_output_aliases={7: 0},
        compiler_params=pltpu.CompilerParams(dimension_semantics=("arbitrary",) * 3, vmem_limit_bytes=VMEM_LIMIT,
                                             has_side_effects=True),
        name="moe_experts",
    )(idx, gate_b, mod, wg, wu, wd, h2, x)


def _final_kernel(x_ref, g_ref, o_ref):
    o_ref[0] = _rms(x_ref[0]) * g_ref[...]


def _final_norm(x3, g, *, c_len, seq):
    n_batch, _, d = x3.shape
    tr = _pick(math.gcd(c_len, seq), (256, 128, 64, 32, 16, 8))
    off = c_len // tr
    return pl.pallas_call(
        _final_kernel, grid=(n_batch, seq // tr),
        in_specs=[pl.BlockSpec((1, tr, d), lambda b, i: (b, i + off, 0)), pl.BlockSpec((1, d), lambda b, i: (0, 0))],
        out_specs=pl.BlockSpec((1, tr, d), lambda b, i: (b, i, 0)),
        out_shape=jax.ShapeDtypeStruct((n_batch, seq, d), F32),
        compiler_params=_params(2), name="final_norm",
    )(x3, g.reshape(1, d))


def _block_ones(n, blk):
    i = jnp.arange(n)
    return (i[:, None] // blk == i[None, :] // blk).astype(BF16)


def _rope_tables(seq, c_len):
    rows = seq // GRID_W
    row = jnp.repeat(jnp.arange(rows, dtype=F32), GRID_W)
    col = jnp.tile(jnp.arange(GRID_W, dtype=F32), rows)
    quarter = AT_HEAD_DIM // 4
    freqs = ROPE_THETA ** (-jnp.arange(quarter, dtype=F32) / quarter)
    ang_r, ang_c = row[:, None] * freqs, col[:, None] * freqs
    cos = jnp.concatenate([jnp.cos(ang_r)] * 2 + [jnp.cos(ang_c)] * 2, axis=1)
    sin = jnp.concatenate([-jnp.sin(ang_r), jnp.sin(ang_r), -jnp.sin(ang_c), jnp.sin(ang_c)], axis=1)
    cos = jnp.concatenate([jnp.ones((c_len, AT_HEAD_DIM), F32), cos], axis=0)
    sin = jnp.concatenate([jnp.zeros((c_len, AT_HEAD_DIM), F32), sin], axis=0)
    return cos, sin


def kernel(x, c, ctx, c_ctx, mod_w, mod_b, norm1_g, norm2_g, w_in, rwkv_conv, rwkv_w0, rwkv_w2, rwkv_a0, rwkv_a2, rwkv_g2, rwkv_kk, rwkv_ka, rwkv_rk, rwkv_ln_g, rwkv_ln_b, s5_lam_re, s5_lam_im, s5_log_dt, s5_b_re, s5_b_im, s5_c_re, s5_c_im, s5_d, s5_glu, attn_qn, attn_kn, w_branch, w_out, router, exp_gate, exp_up, exp_down, final_g):
    n_batch, seq, d = x.shape
    c_len = ctx.shape[1]
    depth = mod_w.shape[0]
    tb = c_len + seq
    m = n_batch * tb
    tc = math.gcd(c_len, 256)
    ne = router.shape[2]
    cap_ctx, cap_lat = CAPACITY_FACTOR * c_len // ne, CAPACITY_FACTOR * seq // ne
    rows = cap_ctx + cap_lat

    xa = jnp.concatenate([ctx, x], axis=1).reshape(m, d)
    c_all_t = jnp.concatenate([c, c_ctx[None]], axis=0).T
    mods = _modulation(c_all_t, mod_w, mod_b)
    ones_head = _block_ones(RW_WIDTH, RW_HEAD_DIM)
    tri = (jnp.arange(128)[:, None] <= jnp.arange(128)[None, :]).astype(BF16)
    cos_t, sin_t = _rope_tables(seq, c_len)
    s5_perms = _s5_perms(tc)
    row_base = (jnp.arange(n_batch, dtype=jnp.int32) * tb)[None, :, None]
    seg_base = jnp.concatenate([jnp.full((cap_lat,), c_len, jnp.int32), jnp.zeros((cap_ctx,), jnp.int32)])[None, None, :]

    for l in range(depth):
        mod = mods[l]
        p = _inproj(xa, mod, norm1_g[l], w_in, l, tb=tb, c_len=c_len, n_batch=n_batch)

        r, v, kk, w0, kd0, b0, w1, kd1, b1, bonus, g = _rwkv_prep(
            p, rwkv_conv[l], rwkv_w0[l], rwkv_w2[l], rwkv_a0[l], rwkv_a2[l], rwkv_g2[l], rwkv_kk[l],
            rwkv_ka[l], rwkv_rk[l].reshape(-1), ones_head, tb=tb, c_len=c_len)
        o0, o1 = _rwkv_scan(r, v, kk, w0, kd0, b0, w1, kd1, b1, n_batch=n_batch, tb=tb, c_len=c_len)
        y_rw = _rwkv_finish(o0, o1, bonus, g, rwkv_ln_g[l], rwkv_ln_b[l], ones_head, tb=tb)

        a_rows, pws, bres, bims, cres, cims = [], [], [], [], [], []
        for z in range(2):
            pws += _s5_powers(s5_lam_re[l, z], s5_lam_im[l, z], s5_log_dt[l, z], tc // S5_SEGMENTS)
            ab_re, ab_im, bb_re, bb_im = _s5_discretize(s5_lam_re[l, z], s5_lam_im[l, z], s5_log_dt[l, z],
                                                        s5_b_re[l, z], s5_b_im[l, z])
            a_rows += [ab_re.reshape(1, S5_NX), ab_im.reshape(1, S5_NX)]
            bres.append(_s5_pack_in(bb_re))
            bims.append(_s5_pack_in(bb_im))
            cres.append(_s5_pack_out(s5_c_re[l, z]))
            cims.append(_s5_pack_out(s5_c_im[l, z]))
        stack = lambda ts: jnp.stack(ts).astype(BF16)
        ys0, ys1 = _s5_scan(p, jnp.concatenate(a_rows, axis=0), jnp.stack(pws), s5_perms, stack(bres), stack(bims), stack(cres), stack(cims),
                            n_batch=n_batch, tb=tb, c_len=c_len, tc=tc)
        y_s5 = _s5_finish(ys0, ys1, p, s5_d[l], s5_glu[l].astype(BF16), tb=tb)

        q_r, k_r, v_b = _attn_prep(p, cos_t, sin_t, attn_qn[l], attn_kn[l], tb=tb)
        y_at = _flash(q_r, k_r, v_b, n_batch=n_batch, tb=tb, c_len=c_len)

        mm = _merge(y_rw, y_s5, y_at, p, w_branch[l].astype(BF16), tb=tb)
        x1 = _outproj(mm, w_out[l].astype(BF16), xa, mod, tb=tb, c_len=c_len, n_batch=n_batch)

        h2, aff_t = _router(x1, mod, norm2_g[l], router[l].T, tb=tb, c_len=c_len, n_batch=n_batch)
        pos = _select(aff_t, tri, n_batch=n_batch, tb=tb, c_len=c_len, caps=(cap_ctx, cap_lat))[..., 0]
        idx = pos + seg_base + row_base
        gate = jnp.take_along_axis(aff_t, idx.reshape(ne, -1), axis=1)
        gate_b = jnp.broadcast_to(gate.reshape(ne * n_batch, rows, 1), (ne * n_batch, rows, 128))
        xa = _moe(idx.reshape(-1), gate_b, mod, exp_gate, exp_up, exp_down, l, h2, x1,
                  n_batch=n_batch, rows=rows, cap_lat=cap_lat)

    return _final_norm(xa.reshape(n_batch, tb, d), final_g, c_len=c_len, seq=seq)
```
